```python
import jax, jax.numpy as jnp
from jax import lax
import numpy as np

D_MODEL = 1024
BATCH = 4
SEQ = 4096
DEPTH = 1
DEC_BATCH = 128
DEC_SEQ = 4
PAST_LEN = 16384
PAGE_SIZE = 128

C_CONV = D_MODEL
CONV_WIDTH = 31
N_HEADS = 8
QK_NOPE = 128
QK_ROPE = 64
V_DIM = 128
Q_LORA = 384
KV_LORA = 256
ROPE_THETA = 10000.0
Q_BLOCK = 128
ATTN_SCALE = (QK_NOPE + QK_ROPE) ** -0.5
D_FF = 2816
FFN_CONV_WIDTH = 3
DN_ALPHA = (2.0 * DEPTH) ** 0.25
DN_BETA = (8.0 * DEPTH) ** -0.25
LN_EPS = 1e-5
RMS_EPS = 1e-6
SPLIT_Q = 2 * C_CONV
SPLIT_KV = SPLIT_Q + Q_LORA
SPLIT_KR = SPLIT_KV + KV_LORA
SPLIT_GATE = SPLIT_KR + QK_ROPE
IN_WIDTH = SPLIT_GATE + 2 * D_MODEL
POOL_NUM, POOL_DEN = 5, 4

kernel_name = 'hybrid_conformer_mla_convffn_deepnorm_step'


def _layer_norm(x, g, b):
    xf = x.astype(jnp.float32)
    mu = jnp.mean(xf, axis=-1, keepdims=True)
    var = jnp.mean(jnp.square(xf - mu), axis=-1, keepdims=True)
    return ((xf - mu) * lax.rsqrt(var + LN_EPS)).astype(x.dtype) * g + b


def _rms_norm(x, g):
    xf = x.astype(jnp.float32)
    ms = jnp.mean(jnp.square(xf), axis=-1, keepdims=True)
    return (xf * lax.rsqrt(ms + RMS_EPS)).astype(x.dtype) * g


def _causal_dwconv(prev, u, w, b):
    ext = jnp.concatenate([prev.astype(u.dtype), u], axis=1)
    y = lax.conv_general_dilated(ext, w[:, None, :].astype(u.dtype), window_strides=(1,), padding='VALID',
                                 dimension_numbers=('NWC', 'WIO', 'NWC'), feature_group_count=u.shape[-1])
    return y + b, ext[:, -(w.shape[0] - 1):]


def _rope_tables(positions):
    inv = 1.0 / (ROPE_THETA ** (jnp.arange(0, QK_ROPE, 2, dtype=jnp.float32) / QK_ROPE))
    ang = positions.astype(jnp.float32)[:, None] * inv[None, :]
    return jnp.cos(ang), jnp.sin(ang)


def _apply_rope(x, cos, sin):
    x1, x2 = jnp.split(x, 2, axis=-1)
    cos = cos.astype(x.dtype)
    sin = sin.astype(x.dtype)
    return jnp.concatenate([x1 * cos - x2 * sin, x1 * sin + x2 * cos], axis=-1)


def _prompt_attention(q_lat, q_rope, c_kv, k_rope):
    B, S = q_lat.shape[:2]
    nb = S // Q_BLOCK
    ql = jnp.moveaxis(q_lat.reshape(B, nb, Q_BLOCK, N_HEADS, KV_LORA), 1, 0)
    qr = jnp.moveaxis(q_rope.reshape(B, nb, Q_BLOCK, N_HEADS, QK_ROPE), 1, 0)
    k_pos = jnp.arange(S)

    def block(args):
        i, ql_b, qr_b = args
        s = jnp.einsum('bqhl,bkl->bhqk', ql_b, c_kv) + jnp.einsum('bqhr,bkr->bhqk', qr_b, k_rope)
        s = s.astype(jnp.float32) * ATTN_SCALE
        q_pos = i * Q_BLOCK + jnp.arange(Q_BLOCK)
        s = jnp.where(q_pos[:, None] >= k_pos[None, :], s, -jnp.inf)
        p = jax.nn.softmax(s, axis=-1).astype(c_kv.dtype)
        return jnp.einsum('bhqk,bkl->bqhl', p, c_kv)

    o = lax.map(block, (jnp.arange(nb), ql, qr))
    return jnp.moveaxis(o, 0, 1).reshape(B, S, N_HEADS, KV_LORA)


def _make_sample_attention(pool_c, pool_kr, page_table):
    def attend(q_lat, q_rope, c_new, kr_new):
        T = q_lat.shape[1]
        causal = jnp.tril(jnp.ones((T, T), dtype=bool))

        def one(args):
            ql, qr, cn, krn, pages = args
            c_past = pool_c[pages].reshape(-1, KV_LORA).astype(cn.dtype)
            kr_past = pool_kr[pages].reshape(-1, QK_ROPE).astype(cn.dtype)
            s_past = jnp.einsum('thl,kl->htk', ql, c_past) + jnp.einsum('thr,kr->htk', qr, kr_past)
            s_new = jnp.einsum('thl,sl->hts', ql, cn) + jnp.einsum('thr,sr->hts', qr, krn)
            s_past = s_past.astype(jnp.float32) * ATTN_SCALE
            s_new = jnp.where(causal, s_new.astype(jnp.float32) * ATTN_SCALE, -jnp.inf)
            p = jax.nn.softmax(jnp.concatenate([s_past, s_new], axis=-1), axis=-1).astype(cn.dtype)
            n_past = c_past.shape[0]
            return (jnp.einsum('htk,kl->thl', p[..., :n_past], c_past)
                    + jnp.einsum('hts,sl->thl', p[..., n_past:], cn))

        return lax.map(one, (q_lat, q_rope, c_new, kr_new, page_table))
    return attend


def _layer(x, positions, conv_prev, ffn_prev, attend, p):
    B, T, _ = x.shape
    z = x @ p['w_in'] + p['b_in']
    z_glu, z_q, z_kv, z_kr, z_gate = jnp.split(z, [SPLIT_Q, SPLIT_KV, SPLIT_KR, SPLIT_GATE], axis=-1)
    a, g = jnp.split(z_glu, 2, axis=-1)
    u, conv_state = _causal_dwconv(conv_prev, a * jax.nn.sigmoid(g), p['conv_dw_w'], p['conv_dw_b'])
    y_conv = jax.nn.silu(_layer_norm(u, p['conv_ln_g'], p['conv_ln_b'])) @ p['w_conv_out']
    cos, sin = _rope_tables(positions)
    q = (_rms_norm(z_q, p['q_norm_g']) @ p['w_uq']).reshape(B, T, N_HEADS, QK_NOPE + QK_ROPE)
    q_nope = q[..., :QK_NOPE]
    q_rope = _apply_rope(q[..., QK_NOPE:], cos[:, None, :], sin[:, None, :])
    q_lat = jnp.einsum('bthn,lhn->bthl', q_nope, p['w_uk'])
    c_kv = _rms_norm(z_kv, p['kv_norm_g'])
    k_rope = _apply_rope(z_kr, cos, sin)
    o_lat = attend(q_lat, q_rope, c_kv, k_rope)
    v = jnp.einsum('bthl,lhv->bthv', o_lat, p['w_uv']).reshape(B, T, N_HEADS * V_DIM)
    y_mla = v @ p['w_mla_out']
    g_conv, g_mla = jnp.split(jax.nn.sigmoid(z_gate), 2, axis=-1)
    mix = (g_conv * y_conv + g_mla * y_mla) @ p['w_mix_out']
    x1 = _layer_norm(DN_ALPHA * x + mix, p['ln1_g'], p['ln1_b'])
    h, ffn_state = _causal_dwconv(ffn_prev, x1 @ p['w_up'], p['ffn_dw_w'], p['ffn_dw_b'])
    hg, hv = jnp.split(h, 2, axis=-1)
    f = (jax.nn.silu(hg) * hv) @ p['w_down']
    y = _layer_norm(DN_ALPHA * x1 + f, p['ln2_g'], p['ln2_b'])
    return y, c_kv, k_rope, conv_state, ffn_state


def setup_inputs(seed: int = 0) -> dict:
    key = jax.random.key(seed)
    ks = jax.random.split(key, 40)
    n_pages = PAST_LEN // PAGE_SIZE
    n_pool = (DEC_BATCH * n_pages * POOL_NUM) // POOL_DEN

    def w(k, shape, fan_in, scale=1.0):
        return jax.random.normal(k, (DEPTH,) + shape, jnp.float32) * (fan_in ** -0.5) * scale

    def gain(k, shape):
        return 1.0 + 0.05 * jax.random.normal(k, (DEPTH,) + shape, jnp.float32)

    def bias(k, shape):
        return 0.02 * jax.random.normal(k, (DEPTH,) + shape, jnp.float32)

    page_table = jax.random.permutation(ks[0], n_pool)[:DEC_BATCH * n_pages].reshape(DEC_BATCH, n_pages).astype(jnp.int32)
    return {
        'x_prompt': jax.random.normal(ks[1], (BATCH, SEQ, D_MODEL), jnp.float32),
        'x_sample': jax.random.normal(ks[2], (DEC_BATCH, DEC_SEQ, D_MODEL), jnp.float32),
        'cache_kv_latent': jax.random.normal(ks[3], (DEPTH, n_pool, PAGE_SIZE, KV_LORA), jnp.float32),
        'cache_k_rope': jax.random.normal(ks[4], (DEPTH, n_pool, PAGE_SIZE, QK_ROPE), jnp.float32),
        'state_conv': jax.random.normal(ks[5], (DEPTH, DEC_BATCH, CONV_WIDTH - 1, C_CONV), jnp.float32),
        'state_ffn_conv': jax.random.normal(ks[6], (DEPTH, DEC_BATCH, FFN_CONV_WIDTH - 1, 2 * D_FF), jnp.float32),
        'page_table': page_table,
        'w_in': w(ks[7], (D_MODEL, IN_WIDTH), D_MODEL),
        'b_in': bias(ks[8], (IN_WIDTH,)),
        'conv_dw_w': w(ks[9], (CONV_WIDTH, C_CONV), CONV_WIDTH),
        'conv_dw_b': bias(ks[10], (C_CONV,)),
        'conv_ln_g': gain(ks[11], (C_CONV,)),
        'conv_ln_b': bias(ks[12], (C_CONV,)),
        'w_conv_out': w(ks[13], (C_CONV, D_MODEL), C_CONV),
        'q_norm_g': gain(ks[14], (Q_LORA,)),
        'w_uq': w(ks[15], (Q_LORA, N_HEADS * (QK_NOPE + QK_ROPE)), Q_LORA),
        'kv_norm_g': gain(ks[16], (KV_LORA,)),
        'w_uk': w(ks[17], (KV_LORA, N_HEADS, QK_NOPE), KV_LORA),
        'w_uv': w(ks[18], (KV_LORA, N_HEADS, V_DIM), KV_LORA, DN_BETA),
        'w_mla_out': w(ks[19], (N_HEADS * V_DIM, D_MODEL), N_HEADS * V_DIM),
        'w_mix_out': w(ks[20], (D_MODEL, D_MODEL), D_MODEL, DN_BETA),
        'ln1_g': gain(ks[21], (D_MODEL,)),
        'ln1_b': bias(ks[22], (D_MODEL,)),
        'w_up': w(ks[23], (D_MODEL, 2 * D_FF), D_MODEL),
        'ffn_dw_w': w(ks[24], (FFN_CONV_WIDTH, 2 * D_FF), FFN_CONV_WIDTH),
        'ffn_dw_b': bias(ks[25], (2 * D_FF,)),
        'w_down': w(ks[26], (D_FF, D_MODEL), D_FF, DN_BETA),
        'ln2_g': gain(ks[27], (D_MODEL,)),
        'ln2_b': bias(ks[28], (D_MODEL,)),
    }


def reference(x_prompt, x_sample, cache_kv_latent, cache_k_rope, state_conv, state_ffn_conv, page_table,
              w_in, b_in, conv_dw_w, conv_dw_b, conv_ln_g, conv_ln_b, w_conv_out, q_norm_g, w_uq, kv_norm_g,
              w_uk, w_uv, w_mla_out, w_mix_out, ln1_g, ln1_b, w_up, ffn_dw_w, ffn_dw_b, w_down, ln2_g, ln2_b):
    B, S, _ = x_prompt.shape
    T = x_sample.shape[1]
    n_pages = page_table.shape[1]
    past_len = n_pages * PAGE_SIZE
    pos_prompt = jnp.arange(S, dtype=jnp.int32)
    pos_sample = past_len + jnp.arange(T, dtype=jnp.int32)
    h_p, h_s = x_prompt, x_sample
    c_p, kr_p, cv_p, ff_p = [], [], [], []
    c_s, kr_s, cv_s, ff_s = [], [], [], []
    for l in range(DEPTH):
        p = dict(w_in=w_in[l], b_in=b_in[l], conv_dw_w=conv_dw_w[l], conv_dw_b=conv_dw_b[l],
                 conv_ln_g=conv_ln_g[l], conv_ln_b=conv_ln_b[l], w_conv_out=w_conv_out[l],
                 q_norm_g=q_norm_g[l], w_uq=w_uq[l], kv_norm_g=kv_norm_g[l], w_uk=w_uk[l], w_uv=w_uv[l],
                 w_mla_out=w_mla_out[l], w_mix_out=w_mix_out[l], ln1_g=ln1_g[l], ln1_b=ln1_b[l],
                 w_up=w_up[l], ffn_dw_w=ffn_dw_w[l], ffn_dw_b=ffn_dw_b[l], w_down=w_down[l],
                 ln2_g=ln2_g[l], ln2_b=ln2_b[l])
        conv0 = jnp.zeros((B, CONV_WIDTH - 1, C_CONV), h_p.dtype)
        ffn0 = jnp.zeros((B, FFN_CONV_WIDTH - 1, 2 * D_FF), h_p.dtype)
        h_p, ckv, kr, cst, fst = _layer(h_p, pos_prompt, conv0, ffn0, _prompt_attention, p)
        c_p.append(ckv); kr_p.append(kr); cv_p.append(cst); ff_p.append(fst)
        attend = _make_sample_attention(cache_kv_latent[l], cache_k_rope[l], page_table)
        h_s, ckv, kr, cst, fst = _layer(h_s, pos_sample, state_conv[l], state_ffn_conv[l], attend, p)
        c_s.append(ckv); kr_s.append(kr); cv_s.append(cst); ff_s.append(fst)
    return (h_p, h_s,
            jnp.stack(c_p, 0), jnp.stack(kr_p, 0), jnp.stack(cv_p, 0), jnp.stack(ff_p, 0),
            jnp.stack(c_s, 0), jnp.stack(kr_s, 0), jnp.stack(cv_s, 0), jnp.stack(ff_s, 0))
```

```python
import functools

import jax
import jax.numpy as jnp
from jax import lax
from jax.experimental import pallas as pl
from jax.experimental.pallas import tpu as pltpu

LN_EPS = 1e-5
RMS_EPS = 1e-6
ROPE_THETA = 10000.0
NEG_INF = float("-inf")

BF16 = jnp.bfloat16
F32 = jnp.float32

VMEM_LIMIT_BYTES = 56 * 1024 * 1024
SUBLANES = 8


def _dot(a, b):
    return jnp.dot(a, b, preferred_element_type=F32)


def _dot_nt(a, b):
    return lax.dot_general(a, b, (((1,), (1,)), ((), ())), preferred_element_type=F32)


def _layer_norm(x, g, b):
    mu = jnp.mean(x, axis=-1, keepdims=True)
    xc = x - mu
    var = jnp.mean(xc * xc, axis=-1, keepdims=True)
    return xc * lax.rsqrt(var + LN_EPS) * g + b


def _rms_norm(x, g):
    ms = jnp.mean(x * x, axis=-1, keepdims=True)
    return x * lax.rsqrt(ms + RMS_EPS) * g


def _params(*sem):
    return pltpu.CompilerParams(dimension_semantics=sem, vmem_limit_bytes=VMEM_LIMIT_BYTES)


def _const_spec(shape):
    nd = len(shape)
    return pl.BlockSpec(shape, lambda *_: (0,) * nd)


def _in_proj_body(x_ref, cs_ref, sn_ref, wglu_ref, bglu_ref, wq_ref, bq_ref, wkv_ref, bkv_ref,
                  wkr_ref, bkr_ref, wgt_ref, bgt_ref, qg_ref, kvg_ref, wuqn_ref, wuqr_ref, wuk_ref,
                  glu_ref, ckv_ref, kr_ref, kcat_ref, q_ref, gate_ref, *, scale):
    n_heads, nope, kv_lora = wuk_ref.shape
    rope = kr_ref.shape[-1]
    half = rope // 2
    c_conv = glu_ref.shape[-1]

    xb = x_ref[...].astype(BF16)

    zg = _dot(xb, wglu_ref[...]) + bglu_ref[...]
    glu_ref[...] = zg[:, :c_conv] * jax.nn.sigmoid(zg[:, c_conv:])

    gate_ref[...] = jax.nn.sigmoid(_dot(xb, wgt_ref[...]) + bgt_ref[...])

    c_kv = _rms_norm(_dot(xb, wkv_ref[...]) + bkv_ref[...], kvg_ref[...])
    ckv_ref[...] = c_kv
    kcat_ref[:, :kv_lora] = c_kv.astype(BF16)
    cs = cs_ref[...]
    sn = sn_ref[...]
    zkr = _dot(xb, wkr_ref[...]) + bkr_ref[...]
    zkr_rot = jnp.concatenate([zkr[:, half:], zkr[:, :half]], axis=-1)
    k_rope = zkr * cs[:, :rope] + zkr_rot * sn[:, :rope]
    kr_ref[...] = k_rope
    kcat_ref[:, kv_lora:] = k_rope.astype(BF16)

    qn = _rms_norm(_dot(xb, wq_ref[...]) + bq_ref[...], qg_ref[...]).astype(BF16)
    q_nope = _dot(qn, wuqn_ref[...])
    q_rope = _dot(qn, wuqr_ref[...])
    width = q_rope.shape[-1]
    lane = lax.broadcasted_iota(jnp.int32, q_rope.shape, 1)
    first_half = (lane % rope) < half
    q_rot = jnp.where(first_half, pltpu.roll(q_rope, width - half, 1), pltpu.roll(q_rope, half, 1))
    q_rope = (q_rope * cs + q_rot * sn) * scale
    for h in range(n_heads):
        q_lat = _dot(q_nope[:, h * nope:(h + 1) * nope].astype(BF16), wuk_ref[h]) * scale
        q_ref[h, :, :kv_lora] = q_lat.astype(BF16)
        q_ref[h, :, kv_lora:] = q_rope[:, h * rope:(h + 1) * rope].astype(BF16)


def _in_proj(x, cs, sn, w, *, tm, scale):
    n, d = x.shape
    n_heads, nope, kv_lora = w["w_ukT"].shape
    rope = w["w_kr"].shape[1]
    c_conv = w["w_glu"].shape[1] // 2
    n_tab = cs.shape[0] // tm
    row = lambda i: (i, 0)
    weights = [w["w_glu"], w["b_glu"], w["w_q"], w["b_q"], w["w_kv"], w["b_kv"], w["w_kr"], w["b_kr"],
               w["w_gate"], w["b_gate"], w["q_norm_g"], w["kv_norm_g"], w["w_uq_nope"], w["w_uq_rope"],
               w["w_ukT"]]
    in_specs = [pl.BlockSpec((tm, d), row),
                pl.BlockSpec((tm, cs.shape[1]), lambda i: (i % n_tab, 0)),
                pl.BlockSpec((tm, sn.shape[1]), lambda i: (i % n_tab, 0))]
    in_specs += [_const_spec(a.shape) for a in weights]
    out_shape = [jax.ShapeDtypeStruct((n, c_conv), F32),
                 jax.ShapeDtypeStruct((n, kv_lora), F32),
                 jax.ShapeDtypeStruct((n, rope), F32),
                 jax.ShapeDtypeStruct((n, kv_lora + rope), BF16),
                 jax.ShapeDtypeStruct((n_heads, n, kv_lora + rope), BF16),
                 jax.ShapeDtypeStruct((n, 2 * d), F32)]
    out_specs = [pl.BlockSpec((tm, c_conv), row),
                 pl.BlockSpec((tm, kv_lora), row),
                 pl.BlockSpec((tm, rope), row),
                 pl.BlockSpec((tm, kv_lora + rope), row),
                 pl.BlockSpec((n_heads, tm, kv_lora + rope), lambda i: (0, i, 0)),
                 pl.BlockSpec((tm, 2 * d), row)]
    return pl.pallas_call(
        functools.partial(_in_proj_body, scale=scale),
        grid=(n // tm,), in_specs=in_specs, out_specs=out_specs, out_shape=out_shape,
        compiler_params=_params("parallel"), name="in_proj",
    )(x, cs, sn, *weights)


def _prompt_attn_body(q_ref, k_ref, wuv_ref, v_ref, m_ref, l_ref, acc_ref, *, tile):
    n_heads, kv_lora, v_dim = wuv_ref.shape
    i = pl.program_id(1)
    rows = n_heads * tile
    q = q_ref[...].reshape(rows, q_ref.shape[-1])

    m_ref[...] = jnp.full(m_ref.shape, NEG_INF, F32)
    l_ref[...] = jnp.zeros(l_ref.shape, F32)
    acc_ref[...] = jnp.zeros(acc_ref.shape, F32)

    def step(j, masked):
        start = pl.multiple_of(j * tile, tile)
        k = k_ref[0, pl.ds(start, tile), :]
        s = _dot_nt(q, k)
        if masked:
            q_pos = lax.broadcasted_iota(jnp.int32, (rows, tile), 0) % tile
            k_pos = lax.broadcasted_iota(jnp.int32, (rows, tile), 1)
            s = jnp.where(q_pos >= k_pos, s, NEG_INF)
        m_prev = m_ref[...]
        m_new = jnp.maximum(m_prev, jnp.max(s, axis=-1, keepdims=True))
        alpha = jnp.exp(m_prev - m_new)
        p = jnp.exp(s - m_new)
        l_ref[...] = alpha * l_ref[...] + jnp.sum(p, axis=-1, keepdims=True)
        acc_ref[...] = alpha * acc_ref[...] + _dot(p.astype(BF16), k[:, :kv_lora])
        m_ref[...] = m_new

    def full_step(j, carry):
        step(j, False)
        return carry

    lax.fori_loop(0, i, full_step, 0)
    step(i, True)

    inv_l = 1.0 / l_ref[...]
    for h in range(n_heads):
        o_h = acc_ref[h * tile:(h + 1) * tile, :] * inv_l[h * tile:(h + 1) * tile, :]
        v_ref[:, h * v_dim:(h + 1) * v_dim] = _dot(o_h.astype(BF16), wuv_ref[h]).astype(v_ref.dtype)


def _prompt_attention(q, kcat, w_uv, *, batch, seq, tile):
    n_heads, n, dk = q.shape
    kv_lora, v_dim = w_uv.shape[1], w_uv.shape[2]
    nq = seq // tile
    k3 = kcat.reshape(batch, seq, dk)
    return pl.pallas_call(
        functools.partial(_prompt_attn_body, tile=tile),
        grid=(batch, nq),
        in_specs=[pl.BlockSpec((n_heads, tile, dk), lambda b, i: (0, b * nq + i, 0)),
                  pl.BlockSpec((1, seq, dk), lambda b, i: (b, 0, 0)),
                  _const_spec(w_uv.shape)],
        out_specs=pl.BlockSpec((tile, n_heads * v_dim), lambda b, i: (b * nq + i, 0)),
        out_shape=jax.ShapeDtypeStruct((n, n_heads * v_dim), BF16),
        scratch_shapes=[pltpu.VMEM((n_heads * tile, 1), F32),
                        pltpu.VMEM((n_heads * tile, 1), F32),
                        pltpu.VMEM((n_heads * tile, kv_lora), F32)],
        compiler_params=_params("parallel", "parallel"), name="prompt_attention",
    )(q, k3, w_uv)


def _sample_attn_body(pt_ref, q_ref, knew_ref, poolc_hbm, poolr_hbm, o_ref,
                      cbuf, rbuf, sem, m_ref, l_ref, acc_ref, *, n_chunks, ppc, n_new):
    page = poolc_hbm.shape[1]
    kv_lora = poolc_hbm.shape[2]
    g = pl.program_id(0)
    n_steps = pl.num_programs(0)
    rows = q_ref.shape[1]

    def page_copies(step, slot):
        b = step // n_chunks
        c = step % n_chunks
        copies = []
        for p in range(ppc):
            pid = pt_ref[b, c * ppc + p]
            copies.append(pltpu.make_async_copy(
                poolc_hbm.at[pid], cbuf.at[slot, pl.ds(p * page, page), :], sem.at[0, slot]))
            copies.append(pltpu.make_async_copy(
                poolr_hbm.at[pid], rbuf.at[slot, pl.ds(p * page, page), :], sem.at[1, slot]))
        return copies

    slot = g % 2

    @pl.when(g == 0)
    def _():
        for cp in page_copies(0, 0):
            cp.start()

    @pl.when(g + 1 < n_steps)
    def _():
        for cp in page_copies(g + 1, 1 - slot):
            cp.start()

    c_idx = g % n_chunks

    @pl.when(c_idx == 0)
    def _():
        m_ref[...] = jnp.full(m_ref.shape, NEG_INF, F32)
        l_ref[...] = jnp.zeros(l_ref.shape, F32)
        acc_ref[...] = jnp.zeros(acc_ref.shape, F32)

    for cp in page_copies(g, slot):
        cp.wait()

    q = q_ref[0]
    q_lat = q[:, :kv_lora]
    q_rope = q[:, kv_lora:]

    def update(s, values):
        m_prev = m_ref[...]
        m_new = jnp.maximum(m_prev, jnp.max(s, axis=-1, keepdims=True))
        alpha = jnp.exp(m_prev - m_new)
        p = jnp.exp(s - m_new)
        l_ref[...] = alpha * l_ref[...] + jnp.sum(p, axis=-1, keepdims=True)
        acc_ref[...] = alpha * acc_ref[...] + _dot(p.astype(BF16), values)
        m_ref[...] = m_new

    cb = cbuf[slot].astype(BF16)
    rb = rbuf[slot].astype(BF16)
    update(_dot_nt(q_lat, cb) + _dot_nt(q_rope, rb), cb)

    @pl.when(c_idx == n_chunks - 1)
    def _():
        knew = knew_ref[0]
        s = _dot_nt(q, knew)
        t_q = lax.broadcasted_iota(jnp.int32, s.shape, 0) % n_new
        t_k = lax.broadcasted_iota(jnp.int32, s.shape, 1)
        s = jnp.where(t_k <= t_q, s, NEG_INF)
        update(s, knew[:, :kv_lora])
        o_ref[0] = acc_ref[...] / l_ref[...]


def _sample_attention(q, knew, pool_c, pool_r, page_table, *, n_new, pages_per_chunk):
    bd, rows, dk = q.shape
    n_pages = page_table.shape[1]
    page, kv_lora = pool_c.shape[1], pool_c.shape[2]
    rope = pool_r.shape[2]
    n_chunks = n_pages // pages_per_chunk
    ck = pages_per_chunk * page
    grid_spec = pltpu.PrefetchScalarGridSpec(
        num_scalar_prefetch=1,
        grid=(bd * n_chunks,),
        in_specs=[pl.BlockSpec((1, rows, dk), lambda g, pt: (g // n_chunks, 0, 0)),
                  pl.BlockSpec((1, knew.shape[1], dk), lambda g, pt: (g // n_chunks, 0, 0)),
                  pl.BlockSpec(memory_space=pl.ANY),
                  pl.BlockSpec(memory_space=pl.ANY)],
        out_specs=pl.BlockSpec((1, rows, kv_lora), lambda g, pt: (g // n_chunks, 0, 0)),
        scratch_shapes=[pltpu.VMEM((2, ck, kv_lora), F32),
                        pltpu.VMEM((2, ck, rope), F32),
                        pltpu.SemaphoreType.DMA((2, 2)),
                        pltpu.VMEM((rows, 1), F32),
                        pltpu.VMEM((rows, 1), F32),
                        pltpu.VMEM((rows, kv_lora), F32)])
    return pl.pallas_call(
        functools.partial(_sample_attn_body, n_chunks=n_chunks, ppc=pages_per_chunk, n_new=n_new),
        grid_spec=grid_spec,
        out_shape=jax.ShapeDtypeStruct((bd, rows, kv_lora), F32),
        compiler_params=_params("arbitrary"), name="sample_attention",
    )(page_table, q, knew, pool_c, pool_r)


def _uv_proj_body(o_ref, wuv_ref, v_ref):
    n_heads, _, v_dim = wuv_ref.shape
    for h in range(n_heads):
        v_ref[:, h * v_dim:(h + 1) * v_dim] = _dot(o_ref[h].astype(BF16), wuv_ref[h]).astype(v_ref.dtype)


def _uv_proj(o_lat, w_uv):
    n_heads, n, kv_lora = o_lat.shape
    v_dim = w_uv.shape[2]
    return pl.pallas_call(
        _uv_proj_body,
        grid=(1,),
        in_specs=[_const_spec(o_lat.shape), _const_spec(w_uv.shape)],
        out_specs=_const_spec((n, n_heads * v_dim)),
        out_shape=jax.ShapeDtypeStruct((n, n_heads * v_dim), BF16),
        compiler_params=_params("arbitrary"), name="uv_proj",
    )(o_lat, w_uv)


def _mix_body(*refs, stride, pad, row_chunk, alpha, has_hist):
    if has_hist:
        hist_ref, refs = refs[0], refs[1:]
    (glu_ref, v_ref, gate_ref, x_ref, dww_ref, dwb_ref, lng_ref, lnb_ref, wco_ref, wmo_ref, wmix_ref,
     ln1g_ref, ln1b_ref, x1_ref, ext_ref, uc_ref) = refs
    tm, c = glu_ref.shape
    taps = dww_ref.shape[0]
    d = x_ref.shape[-1]
    i = pl.program_id(1)

    @pl.when(i == 0)
    def _():
        if has_hist:
            ext_ref[0:pad, :] = hist_ref[...]
        else:
            ext_ref[0:pad, :] = jnp.zeros((pad, c), F32)

    @pl.when(i > 0)
    def _():
        ext_ref[0:pad, :] = ext_ref[tm:tm + pad, :]

    ext_ref[pad:pad + tm, :] = glu_ref[...]

    base = pad - (taps - 1) * stride
    bias = dwb_ref[...]
    g = lng_ref[...]
    b = lnb_ref[...]
    for r in range(tm // row_chunk):
        r0 = r * row_chunk
        acc = jnp.broadcast_to(bias, (row_chunk, c))
        for k in range(taps):
            off = base + k * stride + r0
            acc = acc + dww_ref[k:k + 1, :] * ext_ref[off:off + row_chunk, :]
        y = _layer_norm(acc, g, b)
        uc_ref[r0:r0 + row_chunk, :] = (y * jax.nn.sigmoid(y)).astype(BF16)

    y_conv = _dot(uc_ref[...], wco_ref[...])
    y_mla = _dot(v_ref[...], wmo_ref[...])
    gates = gate_ref[...]
    merged = gates[:, :d] * y_conv + gates[:, d:] * y_mla
    mix = _dot(merged.astype(BF16), wmix_ref[...])
    x1_ref[...] = _layer_norm(alpha * x_ref[...] + mix, ln1g_ref[...], ln1b_ref[...])


def _mix(glu, v, gates, x, hist, w, *, n_seq, tm, stride, alpha):
    n, c = glu.shape
    d = x.shape[1]
    taps = w["conv_dw_w"].shape[0]
    pad = -(-(taps - 1) * stride // SUBLANES) * SUBLANES
    tiles = n // n_seq // tm
    row = lambda s, i: (s * tiles + i, 0)
    weights = [w["conv_dw_w"], w["conv_dw_b"], w["conv_ln_g"], w["conv_ln_b"], w["w_conv_out"],
               w["w_mla_out"], w["w_mix_out"], w["ln1_g"], w["ln1_b"]]
    has_hist = hist is not None
    in_specs = [pl.BlockSpec((tm, c), row), pl.BlockSpec((tm, v.shape[1]), row),
                pl.BlockSpec((tm, 2 * d), row), pl.BlockSpec((tm, d), row)]
    in_specs += [_const_spec(a.shape) for a in weights]
    args = [glu, v, gates, x, *weights]
    if has_hist:
        assert hist.shape == (pad, c) and tiles == 1
        in_specs = [_const_spec(hist.shape)] + in_specs
        args = [hist] + args
    else:
        assert tm >= pad
    return pl.pallas_call(
        functools.partial(_mix_body, stride=stride, pad=pad, row_chunk=min(tm, 32), alpha=alpha,
                          has_hist=has_hist),
        grid=(n_seq, tiles), in_specs=in_specs,
        out_specs=pl.BlockSpec((tm, d), row),
        out_shape=jax.ShapeDtypeStruct((n, d), F32),
        scratch_shapes=[pltpu.VMEM((pad + tm, c), F32), pltpu.VMEM((tm, c), BF16)],
        compiler_params=_params("arbitrary", "arbitrary"), name="mix",
    )(*args)


def _ffn_body(*refs, stride, pad, chunk, alpha, has_hist):
    if has_hist:
        hg0_ref, hv0_ref, refs = refs[0], refs[1], refs[2:]
    (x1_ref, wug_ref, wuv_ref, dwg_ref, dwv_ref, dbg_ref, dbv_ref, wdn_ref, ln2g_ref, ln2b_ref,
     y_ref, tailg_ref, tailv_ref, histg_ref, histv_ref, eg_ref, ev_ref, acc_ref) = refs
    tm = x1_ref.shape[0]
    d_ff = wug_ref.shape[1]
    taps = dwg_ref.shape[0]
    i = pl.program_id(1)

    @pl.when(i == 0)
    def _():
        if has_hist:
            histg_ref[...] = hg0_ref[...]
            histv_ref[...] = hv0_ref[...]
        else:
            histg_ref[...] = jnp.zeros(histg_ref.shape, F32)
            histv_ref[...] = jnp.zeros(histv_ref.shape, F32)

    x1 = x1_ref[...]
    x1b = x1.astype(BF16)
    base = pad - (taps - 1) * stride
    acc_ref[...] = jnp.zeros(acc_ref.shape, F32)

    def conv_half(e_ref, hist_ref, wu_ref, dw_ref, db_ref, c0):
        cols = slice(c0, c0 + chunk)
        e_ref[0:pad, :] = hist_ref[:, cols]
        e_ref[pad:pad + tm, :] = _dot(x1b, wu_ref[:, cols])
        hist_ref[:, cols] = e_ref[tm:tm + pad, :]
        h = jnp.broadcast_to(db_ref[:, cols], (tm, chunk))
        for k in range(taps):
            off = base + k * stride
            h = h + dw_ref[k:k + 1, cols] * e_ref[off:off + tm, :]
        return h

    for cidx in range(d_ff // chunk):
        c0 = cidx * chunk
        hg = conv_half(eg_ref, histg_ref, wug_ref, dwg_ref, dbg_ref, c0)
        hv = conv_half(ev_ref, histv_ref, wuv_ref, dwv_ref, dbv_ref, c0)
        act = (hg * jax.nn.sigmoid(hg) * hv).astype(BF16)
        acc_ref[...] += _dot(act, wdn_ref[c0:c0 + chunk, :])

    tailg_ref[0] = histg_ref[...]
    tailv_ref[0] = histv_ref[...]
    y_ref[...] = _layer_norm(alpha * x1 + acc_ref[...], ln2g_ref[...], ln2b_ref[...])


def _ffn(x1, hist_g, hist_v, w, *, n_seq, tm, stride, alpha, chunk):
    n, d = x1.shape
    d_ff = w["w_up_g"].shape[1]
    taps = w["ffn_dw_g"].shape[0]
    pad = -(-(taps - 1) * stride // SUBLANES) * SUBLANES
    tiles = n // n_seq // tm
    row = lambda s, i: (s * tiles + i, 0)
    weights = [w["w_up_g"], w["w_up_v"], w["ffn_dw_g"], w["ffn_dw_v"], w["ffn_db_g"], w["ffn_db_v"],
               w["w_down"], w["ln2_g"], w["ln2_b"]]
    has_hist = hist_g is not None
    in_specs = [pl.BlockSpec((tm, d), row)] + [_const_spec(a.shape) for a in weights]
    args = [x1, *weights]
    if has_hist:
        assert hist_g.shape == (pad, d_ff) and tiles == 1
        in_specs = [_const_spec(hist_g.shape), _const_spec(hist_v.shape)] + in_specs
        args = [hist_g, hist_v] + args
    else:
        assert tm >= pad
    tail_spec = pl.BlockSpec((1, pad, d_ff), lambda s, i: (s, 0, 0))
    return pl.pallas_call(
        functools.partial(_ffn_body, stride=stride, pad=pad, chunk=chunk, alpha=alpha, has_hist=has_hist),
        grid=(n_seq, tiles), in_specs=in_specs,
        out_specs=[pl.BlockSpec((tm, d), row), tail_spec, tail_spec],
        out_shape=[jax.ShapeDtypeStruct((n, d), F32),
                   jax.ShapeDtypeStruct((n_seq, pad, d_ff), F32),
                   jax.ShapeDtypeStruct((n_seq, pad, d_ff), F32)],
        scratch_shapes=[pltpu.VMEM((pad, d_ff), F32), pltpu.VMEM((pad, d_ff), F32),
                        pltpu.VMEM((pad + tm, chunk), F32), pltpu.VMEM((pad + tm, chunk), F32),
                        pltpu.VMEM((tm, d), F32)],
        compiler_params=_params("arbitrary", "arbitrary"), name="conv_ffn",
    )(*args)


def _rope_tables(positions, rope, n_heads):
    half = rope // 2
    inv = 1.0 / (ROPE_THETA ** (jnp.arange(0, rope, 2, dtype=F32) / rope))
    ang = positions.astype(F32)[:, None] * inv[None, :]
    cos, sin = jnp.cos(ang), jnp.sin(ang)
    cs = jnp.tile(jnp.concatenate([cos, cos], axis=-1), (1, n_heads))
    sn = jnp.tile(jnp.concatenate([-sin, sin], axis=-1), (1, n_heads))
    return cs, sn


def _prepare_weights(l, c_conv, w_in, b_in, conv_dw_w, conv_dw_b, conv_ln_g, conv_ln_b, w_conv_out, q_norm_g,
                     w_uq, kv_norm_g, w_uk, w_uv, w_mla_out, w_mix_out, ln1_g, ln1_b, w_up, ffn_dw_w,
                     ffn_dw_b, w_down, ln2_g, ln2_b, rope):
    kv_lora, n_heads, nope = w_uk.shape[1:]
    q_lora = q_norm_g.shape[1]
    d_ff = w_down.shape[1]
    s_q = 2 * c_conv
    s_kv = s_q + q_lora
    s_kr = s_kv + kv_lora
    s_gate = s_kr + rope
    wi, bi = w_in[l], b_in[l][None, :]
    uq = w_uq[l].reshape(q_lora, n_heads, nope + rope)
    row = lambda a: a[l][None, :]
    return {
        "w_glu": wi[:, :s_q].astype(BF16), "b_glu": bi[:, :s_q],
        "w_q": wi[:, s_q:s_kv].astype(BF16), "b_q": bi[:, s_q:s_kv],
        "w_kv": wi[:, s_kv:s_kr].astype(BF16), "b_kv": bi[:, s_kv:s_kr],
        "w_kr": wi[:, s_kr:s_gate].astype(BF16), "b_kr": bi[:, s_kr:s_gate],
        "w_gate": wi[:, s_gate:].astype(BF16), "b_gate": bi[:, s_gate:],
        "q_norm_g": row(q_norm_g), "kv_norm_g": row(kv_norm_g),
        "w_uq_nope": uq[:, :, :nope].reshape(q_lora, n_heads * nope).astype(BF16),
        "w_uq_rope": uq[:, :, nope:].reshape(q_lora, n_heads * rope).astype(BF16),
        "w_ukT": jnp.transpose(w_uk[l], (1, 2, 0)).astype(BF16),
        "w_uv": jnp.transpose(w_uv[l], (1, 0, 2)).astype(BF16),
        "conv_dw_w": conv_dw_w[l], "conv_dw_b": row(conv_dw_b),
        "conv_ln_g": row(conv_ln_g), "conv_ln_b": row(conv_ln_b),
        "w_conv_out": w_conv_out[l].astype(BF16), "w_mla_out": w_mla_out[l].astype(BF16),
        "w_mix_out": w_mix_out[l].astype(BF16), "ln1_g": row(ln1_g), "ln1_b": row(ln1_b),
        "w_up_g": w_up[l][:, :d_ff].astype(BF16), "w_up_v": w_up[l][:, d_ff:].astype(BF16),
        "ffn_dw_g": ffn_dw_w[l][:, :d_ff], "ffn_dw_v": ffn_dw_w[l][:, d_ff:],
        "ffn_db_g": ffn_dw_b[l][None, :d_ff], "ffn_db_v": ffn_dw_b[l][None, d_ff:],
        "w_down": w_down[l].astype(BF16), "ln2_g": row(ln2_g), "ln2_b": row(ln2_b),
    }


def _pick_tile(n, target):
    t = min(n, target)
    while n % t:
        t -= SUBLANES
    return t


def kernel(x_prompt, x_sample, cache_kv_latent, cache_k_rope, state_conv, state_ffn_conv, page_table, w_in, b_in, conv_dw_w, conv_dw_b, conv_ln_g, conv_ln_b, w_conv_out, q_norm_g, w_uq, kv_norm_g, w_uk, w_uv, w_mla_out, w_mix_out, ln1_g, ln1_b, w_up, ffn_dw_w, ffn_dw_b, w_down, ln2_g, ln2_b):
    depth = w_in.shape[0]
    bp, sp, d = x_prompt.shape
    bs, ts, _ = x_sample.shape
    c_conv = state_conv.shape[-1]
    conv_hist = state_conv.shape[2]
    ffn_hist = state_ffn_conv.shape[2]
    kv_lora, n_heads, nope = w_uk.shape[1:]
    rope = cache_k_rope.shape[-1]
    d_ff = w_down.shape[1]
    page = cache_kv_latent.shape[2]
    n_pages = page_table.shape[1]
    past_len = n_pages * page
    alpha = (2.0 * depth) ** 0.25
    scale = float(nope + rope) ** -0.5

    tm_p = _pick_tile(sp, 256)
    tile_attn = _pick_tile(sp, 256)
    n_s = bs * ts
    tm_s = _pick_tile(n_s, 256)
    ffn_chunk = 256 if d_ff % 256 == 0 else 128
    pages_per_chunk = min(n_pages, 32)

    cs_p, sn_p = _rope_tables(jnp.arange(sp, dtype=jnp.int32), rope, n_heads)
    pos_s = past_len + jnp.arange(ts, dtype=jnp.int32)
    cs_s, sn_s = _rope_tables(jnp.repeat(pos_s, bs), rope, n_heads)

    h_p = x_prompt.reshape(bp * sp, d)
    h_s = jnp.transpose(x_sample, (1, 0, 2)).reshape(n_s, d)
    outs = [[] for _ in range(8)]
    for l in range(depth):
        w = _prepare_weights(l, c_conv, w_in, b_in, conv_dw_w, conv_dw_b, conv_ln_g, conv_ln_b, w_conv_out,
                             q_norm_g, w_uq, kv_norm_g, w_uk, w_uv, w_mla_out, w_mix_out, ln1_g, ln1_b,
                             w_up, ffn_dw_w, ffn_dw_b, w_down, ln2_g, ln2_b, rope)

        glu, ckv, kr, kcat, q, gates = _in_proj(h_p, cs_p, sn_p, w, tm=tm_p, scale=scale)
        v = _prompt_attention(q, kcat, w["w_uv"], batch=bp, seq=sp, tile=tile_attn)
        x1 = _mix(glu, v, gates, h_p, None, w, n_seq=bp, tm=tm_p, stride=1, alpha=alpha)
        h_p, tail_g, tail_v = _ffn(x1, None, None, w, n_seq=bp, tm=tm_p, stride=1, alpha=alpha,
                                   chunk=ffn_chunk)
        outs[0].append(ckv.reshape(bp, sp, kv_lora))
        outs[1].append(kr.reshape(bp, sp, rope))
        outs[2].append(glu.reshape(bp, sp, c_conv)[:, sp - conv_hist:, :])
        outs[3].append(jnp.concatenate([tail_g, tail_v], axis=-1)[:, -ffn_hist:, :])

        glu, ckv, kr, kcat, q, gates = _in_proj(h_s, cs_s, sn_s, w, tm=tm_s, scale=scale)
        q_b = jnp.transpose(q.reshape(n_heads, ts, bs, kv_lora + rope), (2, 0, 1, 3))
        q_b = q_b.reshape(bs, n_heads * ts, kv_lora + rope)
        knew = jnp.transpose(kcat.reshape(ts, bs, kv_lora + rope), (1, 0, 2))
        knew = jnp.pad(knew, ((0, 0), (0, 128 - ts), (0, 0)))
        o_lat = _sample_attention(q_b, knew, cache_kv_latent[l], cache_k_rope[l], page_table,
                                  n_new=ts, pages_per_chunk=pages_per_chunk)
        o_lat = jnp.transpose(o_lat.reshape(bs, n_heads, ts, kv_lora), (1, 2, 0, 3))
        v = _uv_proj(o_lat.reshape(n_heads, n_s, kv_lora), w["w_uv"])
        conv_prev = jnp.transpose(state_conv[l], (1, 0, 2))
        x1 = _mix(glu, v, gates, h_s, conv_prev.reshape(conv_hist * bs, c_conv), w,
                  n_seq=1, tm=n_s, stride=bs, alpha=alpha)
        ffn_prev = jnp.transpose(state_ffn_conv[l], (1, 0, 2)).reshape(ffn_hist * bs, 2 * d_ff)
        h_s, tail_g, tail_v = _ffn(x1, ffn_prev[:, :d_ff], ffn_prev[:, d_ff:], w, n_seq=1, tm=n_s,
                                   stride=bs, alpha=alpha, chunk=ffn_chunk)
        to_batch_major = lambda a, t: jnp.transpose(a.reshape(t, bs, a.shape[-1]), (1, 0, 2))
        outs[4].append(to_batch_major(ckv, ts))
        outs[5].append(to_batch_major(kr, ts))
        conv_ext = jnp.concatenate([conv_prev, glu.reshape(ts, bs, c_conv)], axis=0)
        outs[6].append(jnp.transpose(conv_ext[-conv_hist:], (1, 0, 2)))
        ffn_tail = jnp.concatenate([tail_g[0], tail_v[0]], axis=-1)
        outs[7].append(to_batch_major(ffn_tail, ffn_hist))

    y_prompt = h_p.reshape(bp, sp, d)
    y_sample = jnp.transpose(h_s.reshape(ts, bs, d), (1, 0, 2))
    return (y_prompt, y_sample, *[jnp.stack(o, 0) for o in outs])
```

```python
import functools

import jax
import jax.numpy as jnp
from jax import lax
from jax.experimental import pallas as pl
from jax.experimental.pallas import tpu as pltpu

LN_EPS = 1e-5
RMS_EPS = 1e-6
ROPE_THETA = 10000.0
NEG_INF = float("-inf")

BF16 = jnp.bfloat16
F32 = jnp.float32

VMEM_LIMIT_BYTES = 56 * 1024 * 1024
SUBLANES = 8
LANES = 128
LOG2_E = 1.4426950408889634


def _dot(a, b):
    return jnp.dot(a, b, preferred_element_type=F32)


def _dot_nt(a, b):
    return lax.dot_general(a, b, (((1,), (1,)), ((), ())), preferred_element_type=F32)


def _layer_norm(x, g, b):
    mu = jnp.mean(x, axis=-1, keepdims=True)
    xc = x - mu
    var = jnp.mean(xc * xc, axis=-1, keepdims=True)
    return xc * lax.rsqrt(var + LN_EPS) * g + b


def _rms_norm(x, g):
    ms = jnp.mean(x * x, axis=-1, keepdims=True)
    return x * lax.rsqrt(ms + RMS_EPS) * g


def _params(*sem):
    return pltpu.CompilerParams(dimension_semantics=sem, vmem_limit_bytes=VMEM_LIMIT_BYTES)


def _const_spec(shape):
    nd = len(shape)
    return pl.BlockSpec(shape, lambda *_: (0,) * nd)


def _in_proj_body(x_ref, cs_ref, sn_ref, wglu_ref, bglu_ref, wq_ref, bq_ref, wkv_ref, bkv_ref,
                  wkr_ref, bkr_ref, wgt_ref, bgt_ref, qg_ref, kvg_ref, wuqn_ref, wuqr_ref, wuk_ref,
                  glu_ref, ckv_ref, kr_ref, kcat_ref, q_ref, gate_ref, *, scale):
    n_heads, nope, kv_lora = wuk_ref.shape
    rope = kr_ref.shape[-1]
    half = rope // 2
    c_conv = glu_ref.shape[-1]

    xb = x_ref[...].astype(BF16)

    zg = _dot(xb, wglu_ref[...]) + bglu_ref[...]
    glu_ref[...] = zg[:, :c_conv] * jax.nn.sigmoid(zg[:, c_conv:])

    gate_ref[...] = jax.nn.sigmoid(_dot(xb, wgt_ref[...]) + bgt_ref[...])

    c_kv = _rms_norm(_dot(xb, wkv_ref[...]) + bkv_ref[...], kvg_ref[...])
    ckv_ref[...] = c_kv
    kcat_ref[:, :kv_lora] = c_kv.astype(BF16)
    cs = cs_ref[...]
    sn = sn_ref[...]
    zkr = _dot(xb, wkr_ref[...]) + bkr_ref[...]
    zkr_rot = jnp.concatenate([zkr[:, half:], zkr[:, :half]], axis=-1)
    k_rope = zkr * cs[:, :rope] + zkr_rot * sn[:, :rope]
    kr_ref[...] = k_rope
    kcat_ref[:, kv_lora:] = k_rope.astype(BF16)

    qn = _rms_norm(_dot(xb, wq_ref[...]) + bq_ref[...], qg_ref[...]).astype(BF16)
    q_nope = _dot(qn, wuqn_ref[...])
    q_rope = _dot(qn, wuqr_ref[...])
    width = q_rope.shape[-1]
    lane = lax.broadcasted_iota(jnp.int32, q_rope.shape, 1)
    first_half = (lane % rope) < half
    q_rot = jnp.where(first_half, pltpu.roll(q_rope, width - half, 1), pltpu.roll(q_rope, half, 1))
    q_rope = (q_rope * cs + q_rot * sn) * scale
    for h in range(n_heads):
        q_lat = _dot(q_nope[:, h * nope:(h + 1) * nope].astype(BF16), wuk_ref[h]) * scale
        q_ref[h, :, :kv_lora] = q_lat.astype(BF16)
        q_ref[h, :, kv_lora:] = q_rope[:, h * rope:(h + 1) * rope].astype(BF16)


def _in_proj(x, cs, sn, w, *, tm, scale):
    n, d = x.shape
    n_heads, nope, kv_lora = w["w_ukT"].shape
    rope = w["w_kr"].shape[1]
    c_conv = w["w_glu"].shape[1] // 2
    n_tab = cs.shape[0] // tm
    row = lambda i: (i, 0)
    weights = [w["w_glu"], w["b_glu"], w["w_q"], w["b_q"], w["w_kv"], w["b_kv"], w["w_kr"], w["b_kr"],
               w["w_gate"], w["b_gate"], w["q_norm_g"], w["kv_norm_g"], w["w_uq_nope"], w["w_uq_rope"],
               w["w_ukT"]]
    in_specs = [pl.BlockSpec((tm, d), row),
                pl.BlockSpec((tm, cs.shape[1]), lambda i: (i % n_tab, 0)),
                pl.BlockSpec((tm, sn.shape[1]), lambda i: (i % n_tab, 0))]
    in_specs += [_const_spec(a.shape) for a in weights]
    out_shape = [jax.ShapeDtypeStruct((n, c_conv), F32),
                 jax.ShapeDtypeStruct((n, kv_lora), F32),
                 jax.ShapeDtypeStruct((n, rope), F32),
                 jax.ShapeDtypeStruct((n, kv_lora + rope), BF16),
                 jax.ShapeDtypeStruct((n_heads, n, kv_lora + rope), BF16),
                 jax.ShapeDtypeStruct((n, 2 * d), F32)]
    out_specs = [pl.BlockSpec((tm, c_conv), row),
                 pl.BlockSpec((tm, kv_lora), row),
                 pl.BlockSpec((tm, rope), row),
                 pl.BlockSpec((tm, kv_lora + rope), row),
                 pl.BlockSpec((n_heads, tm, kv_lora + rope), lambda i: (0, i, 0)),
                 pl.BlockSpec((tm, 2 * d), row)]
    return pl.pallas_call(
        functools.partial(_in_proj_body, scale=scale),
        grid=(n // tm,), in_specs=in_specs, out_specs=out_specs, out_shape=out_shape,
        compiler_params=_params("parallel"), name="in_proj",
    )(x, cs, sn, *weights)


def _lane_tile(x, n):
    return x if n == 1 else jnp.concatenate([x] * n, axis=-1)


def _lane_fold(p, lanes):
    out = p[:, :lanes]
    for c in range(1, p.shape[-1] // lanes):
        out = out + p[:, c * lanes:(c + 1) * lanes]
    return out


def _prompt_attn_body(q_ref, k_ref, wuv_ref, v_ref, s_ref, p_ref, m_ref, l_ref, acc_ref, *, tq, tk, rb, groups):
    n_heads, kv_lora, v_dim = wuv_ref.shape
    lanes = m_ref.shape[-1]
    i = pl.program_id(1)
    rows = n_heads * tq
    ratio = tk // tq
    q = q_ref[...].reshape(rows, q_ref.shape[-1])

    m_ref[...] = jnp.full(m_ref.shape, NEG_INF, F32)
    l_ref[...] = jnp.zeros(l_ref.shape, F32)
    acc_ref[...] = jnp.zeros(acc_ref.shape, F32)

    def softmax_block(sl, r0, width, masked):
        s = s_ref[sl, :width]
        if masked:
            q_pos = lax.broadcasted_iota(jnp.int32, (rb, width), 0) + r0 % tq
            k_pos = lax.broadcasted_iota(jnp.int32, (rb, width), 1)
            s = jnp.where(q_pos >= k_pos, s, NEG_INF)
        m_prev = m_ref[sl, :]
        m_new = jnp.maximum(m_prev, jnp.max(s, axis=-1, keepdims=True))
        alpha = jnp.exp2(m_prev - m_new)
        p = jnp.exp2(s - _lane_tile(m_new, width // lanes))
        l_ref[sl, :] = alpha * l_ref[sl, :] + _lane_fold(p, lanes)
        m_ref[sl, :] = m_new
        p_ref[sl, :width] = p.astype(BF16)
        acc_ref[sl, :] = acc_ref[sl, :] * _lane_tile(alpha, kv_lora // lanes)

    def step(start, width, masked):
        k = k_ref[0, pl.ds(start, width), :]
        gr = rows // groups
        for g in range(groups):
            s_ref[g * gr:(g + 1) * gr, :width] = _dot_nt(q[g * gr:(g + 1) * gr], k)
        for g in range(groups):
            for r0 in range(g * gr, (g + 1) * gr, rb):
                softmax_block(slice(r0, r0 + rb), r0, width, masked)
            acc_ref[g * gr:(g + 1) * gr, :] += _dot(p_ref[g * gr:(g + 1) * gr, :width], k[:, :kv_lora])

    def wide_step(j, carry):
        step(pl.multiple_of(j * tk, tk), tk, False)
        return carry

    n_wide = lax.div(i, ratio)
    lax.fori_loop(0, n_wide, wide_step, 0)
    if ratio > 1:
        def narrow_step(j, carry):
            step(pl.multiple_of(n_wide * tk + j * tq, tq), tq, False)
            return carry

        lax.fori_loop(0, lax.rem(i, ratio), narrow_step, 0)
    step(pl.multiple_of(i * tq, tq), tq, True)

    inv_l = 1.0 / jnp.sum(l_ref[...], axis=-1, keepdims=True)
    for h in range(n_heads):
        o_h = acc_ref[h * tq:(h + 1) * tq, :] * inv_l[h * tq:(h + 1) * tq, :]
        v_ref[:, h * v_dim:(h + 1) * v_dim] = _dot(o_h.astype(BF16), wuv_ref[h]).astype(v_ref.dtype)


def _prompt_attention(q, kcat, w_uv, *, batch, seq, tq, tk):
    n_heads, n, dk = q.shape
    kv_lora, v_dim = w_uv.shape[1], w_uv.shape[2]
    nq = seq // tq
    rows = n_heads * tq
    k3 = kcat.reshape(batch, seq, dk)
    return pl.pallas_call(
        functools.partial(_prompt_attn_body, tq=tq, tk=tk, rb=32, groups=4),
        grid=(batch, nq),
        in_specs=[pl.BlockSpec((n_heads, tq, dk), lambda b, i: (0, b * nq + i, 0)),
                  pl.BlockSpec((1, seq, dk), lambda b, i: (b, 0, 0)),
                  _const_spec(w_uv.shape)],
        out_specs=pl.BlockSpec((tq, n_heads * v_dim), lambda b, i: (b * nq + i, 0)),
        out_shape=jax.ShapeDtypeStruct((n, n_heads * v_dim), BF16),
        scratch_shapes=[pltpu.VMEM((rows, tk), F32),
                        pltpu.VMEM((rows, tk), BF16),
                        pltpu.VMEM((rows, LANES), F32),
                        pltpu.VMEM((rows, LANES), F32),
                        pltpu.VMEM((rows, kv_lora), F32)],
        compiler_params=_params("parallel", "parallel"), name="prompt_attention",
    )(q, k3, w_uv)


def _sample_attn_body(pt_ref, q_ref, knew_ref, poolc_hbm, poolr_hbm, o_ref,
                      cbuf, rbuf, sem, m_ref, l_ref, acc_ref, *, ppc, n_new):
    page = poolc_hbm.shape[1]
    kv_lora = poolc_hbm.shape[2]
    b = pl.program_id(0)
    c = pl.program_id(1)
    n_seq = pl.num_programs(0)
    n_chunks = pl.num_programs(1)
    slot = jnp.bitwise_and(b * n_chunks + c, 1)

    def page_copies(b_, c_, slot_):
        copies = []
        for p in range(ppc):
            pid = pt_ref[b_, c_ * ppc + p]
            copies.append(pltpu.make_async_copy(
                poolc_hbm.at[pid], cbuf.at[slot_, pl.ds(p * page, page), :], sem.at[0, slot_]))
            copies.append(pltpu.make_async_copy(
                poolr_hbm.at[pid], rbuf.at[slot_, :, pl.ds(p * page, page)], sem.at[1, slot_]))
        return copies

    @pl.when((b == 0) & (c == 0))
    def _():
        for cp in page_copies(0, 0, 0):
            cp.start()

    last_chunk = c == n_chunks - 1

    @pl.when(jnp.logical_not(last_chunk & (b == n_seq - 1)))
    def _():
        b_next = jnp.where(last_chunk, b + 1, b)
        c_next = jnp.where(last_chunk, 0, c + 1)
        for cp in page_copies(b_next, c_next, 1 - slot):
            cp.start()

    @pl.when(c == 0)
    def _():
        m_ref[...] = jnp.full(m_ref.shape, NEG_INF, F32)
        l_ref[...] = jnp.zeros(l_ref.shape, F32)
        acc_ref[...] = jnp.zeros(acc_ref.shape, F32)

    for cp in page_copies(b, c, slot):
        cp.wait()

    q = q_ref[0]
    q_lat = q[:, :kv_lora]
    q_rope = q[:, kv_lora:]

    def update(s, values):
        m_prev = m_ref[...]
        m_new = jnp.maximum(m_prev, jnp.max(s, axis=-1, keepdims=True))
        alpha = jnp.exp2(m_prev - m_new)
        p = jnp.exp2(s - m_new)
        l_ref[...] = alpha * l_ref[...] + jnp.sum(p, axis=-1, keepdims=True)
        acc_ref[...] = alpha * acc_ref[...] + _dot(p.astype(BF16), values)
        m_ref[...] = m_new

    cb = cbuf[slot].astype(BF16)
    rb = rbuf[slot].astype(BF16)
    update(_dot_nt(q_lat, cb) + _dot(q_rope, rb), cb)

    @pl.when(last_chunk)
    def _():
        knew = knew_ref[0]
        s = _dot_nt(q, knew)
        t_q = lax.rem(lax.broadcasted_iota(jnp.int32, s.shape, 0), n_new)
        t_k = lax.broadcasted_iota(jnp.int32, s.shape, 1)
        s = jnp.where(t_k <= t_q, s, NEG_INF)
        update(s, knew[:, :kv_lora])
        o_ref[0] = acc_ref[...] / l_ref[...]


def _sample_attention(q, knew, pool_c, pool_rt, page_table, *, n_new, pages_per_chunk):
    bd, rows, dk = q.shape
    n_pages = page_table.shape[1]
    page, kv_lora = pool_c.shape[1], pool_c.shape[2]
    rope = pool_rt.shape[1]
    n_chunks = n_pages // pages_per_chunk
    ck = pages_per_chunk * page
    grid_spec = pltpu.PrefetchScalarGridSpec(
        num_scalar_prefetch=1,
        grid=(bd, n_chunks),
        in_specs=[pl.BlockSpec((1, rows, dk), lambda b, c, pt: (b, 0, 0)),
                  pl.BlockSpec((1, knew.shape[1], dk), lambda b, c, pt: (b, 0, 0)),
                  pl.BlockSpec(memory_space=pl.ANY),
                  pl.BlockSpec(memory_space=pl.ANY)],
        out_specs=pl.BlockSpec((1, rows, kv_lora), lambda b, c, pt: (b, 0, 0)),
        scratch_shapes=[pltpu.VMEM((2, ck, kv_lora), F32),
                        pltpu.VMEM((2, rope, ck), F32),
                        pltpu.SemaphoreType.DMA((2, 2)),
                        pltpu.VMEM((rows, 1), F32),
                        pltpu.VMEM((rows, 1), F32),
                        pltpu.VMEM((rows, kv_lora), F32)])
    return pl.pallas_call(
        functools.partial(_sample_attn_body, ppc=pages_per_chunk, n_new=n_new),
        grid_spec=grid_spec,
        out_shape=jax.ShapeDtypeStruct((bd, rows, kv_lora), F32),
        compiler_params=_params("arbitrary", "arbitrary"), name="sample_attention",
    )(page_table, q, knew, pool_c, pool_rt)


def _uv_proj_body(o_ref, wuv_ref, v_ref):
    n_heads, _, v_dim = wuv_ref.shape
    for h in range(n_heads):
        v_ref[:, h * v_dim:(h + 1) * v_dim] = _dot(o_ref[h].astype(BF16), wuv_ref[h]).astype(v_ref.dtype)


def _uv_proj(o_lat, w_uv):
    n_heads, n, kv_lora = o_lat.shape
    v_dim = w_uv.shape[2]
    return pl.pallas_call(
        _uv_proj_body,
        grid=(1,),
        in_specs=[_const_spec(o_lat.shape), _const_spec(w_uv.shape)],
        out_specs=_const_spec((n, n_heads * v_dim)),
        out_shape=jax.ShapeDtypeStruct((n, n_heads * v_dim), BF16),
        compiler_params=_params("arbitrary"), name="uv_proj",
    )(o_lat, w_uv)


def _dwconv_block(ext_ref, w_ref, b_ref, r0, rc, c0, cw, base, stride, taps):
    cols = slice(c0, c0 + cw)
    groups = {}
    for k in range(taps):
        off = base + k * stride
        groups.setdefault(off % SUBLANES, []).append((k, off - off % SUBLANES))
    top = max(groups)
    cur = None
    for res in range(top, -1, -1):
        n_rows = rc + SUBLANES if (res > 0 and top > 0) else rc
        z = None
        for k, off in groups.get(res, []):
            term = w_ref[k:k + 1, cols] * ext_ref[r0 + off:r0 + off + n_rows, cols]
            z = term if z is None else z + term
        if cur is not None:
            shifted = pltpu.roll(cur, cur.shape[0] - 1, 0)[:n_rows]
            z = shifted if z is None else z + shifted
        cur = z
    return cur + b_ref[:, cols]


def _mix_body(*refs, stride, pad, row_chunk, col_chunk, ln_chunk, alpha, has_hist):
    if has_hist:
        hist_ref, refs = refs[0], refs[1:]
    (glu_ref, v_ref, gate_ref, x_ref, dww_ref, dwb_ref, lng_ref, lnb_ref, wco_ref, wmo_ref, wmix_ref,
     ln1g_ref, ln1b_ref, x1_ref, ext_ref, u_ref, uc_ref) = refs
    tm, c = glu_ref.shape
    taps = dww_ref.shape[0]
    d = x_ref.shape[-1]
    i = pl.program_id(1)

    @pl.when(i == 0)
    def _():
        if has_hist:
            ext_ref[0:pad, :] = hist_ref[...]
        else:
            ext_ref[0:pad, :] = jnp.zeros((pad, c), F32)

    @pl.when(i > 0)
    def _():
        ext_ref[0:pad, :] = ext_ref[tm:tm + pad, :]

    ext_ref[pad:pad + tm, :] = glu_ref[...]

    base = pad - (taps - 1) * stride
    for r0 in range(0, tm, row_chunk):
        for c0 in range(0, c, col_chunk):
            u_ref[r0:r0 + row_chunk, c0:c0 + col_chunk] = _dwconv_block(
                ext_ref, dww_ref, dwb_ref, r0, row_chunk, c0, col_chunk, base, stride, taps)

    g = lng_ref[...]
    b = lnb_ref[...]
    for r0 in range(0, tm, ln_chunk):
        y = _layer_norm(u_ref[r0:r0 + ln_chunk, :], g, b)
        uc_ref[r0:r0 + ln_chunk, :] = (y * jax.nn.sigmoid(y)).astype(BF16)

    y_conv = _dot(uc_ref[...], wco_ref[...])
    y_mla = _dot(v_ref[...], wmo_ref[...])
    gates = gate_ref[...]
    merged = gates[:, :d] * y_conv + gates[:, d:] * y_mla
    mix = _dot(merged.astype(BF16), wmix_ref[...])
    x1_ref[...] = _layer_norm(alpha * x_ref[...] + mix, ln1g_ref[...], ln1b_ref[...])


def _mix(glu, v, gates, x, hist, w, *, n_seq, tm, stride, alpha):
    n, c = glu.shape
    d = x.shape[1]
    taps = w["conv_dw_w"].shape[0]
    pad = -(-(taps - 1) * stride // SUBLANES) * SUBLANES
    tiles = n // n_seq // tm
    row = lambda s, i: (s * tiles + i, 0)
    weights = [w["conv_dw_w"], w["conv_dw_b"], w["conv_ln_g"], w["conv_ln_b"], w["w_conv_out"],
               w["w_mla_out"], w["w_mix_out"], w["ln1_g"], w["ln1_b"]]
    has_hist = hist is not None
    in_specs = [pl.BlockSpec((tm, c), row), pl.BlockSpec((tm, v.shape[1]), row),
                pl.BlockSpec((tm, 2 * d), row), pl.BlockSpec((tm, d), row)]
    in_specs += [_const_spec(a.shape) for a in weights]
    args = [glu, v, gates, x, *weights]
    if has_hist:
        assert hist.shape == (pad, c) and tiles == 1
        in_specs = [_const_spec(hist.shape)] + in_specs
        args = [hist] + args
    else:
        assert tm >= pad
    return pl.pallas_call(
        functools.partial(_mix_body, stride=stride, pad=pad, row_chunk=_pick_tile(tm, 64),
                          col_chunk=_pick_tile(c, 2 * LANES), ln_chunk=_pick_tile(tm, 32), alpha=alpha,
                          has_hist=has_hist),
        grid=(n_seq, tiles), in_specs=in_specs,
        out_specs=pl.BlockSpec((tm, d), row),
        out_shape=jax.ShapeDtypeStruct((n, d), F32),
        scratch_shapes=[pltpu.VMEM((pad + tm, c), F32), pltpu.VMEM((tm, c), F32),
                        pltpu.VMEM((tm, c), BF16)],
        compiler_params=_params("arbitrary", "arbitrary"), name="mix",
    )(*args)


def _ffn_body(*refs, stride, pad, chunk, alpha, has_hist):
    if has_hist:
        hg0_ref, hv0_ref, refs = refs[0], refs[1], refs[2:]
    (x1_ref, wug_ref, wuv_ref, dwg_ref, dwv_ref, dbg_ref, dbv_ref, wdn_ref, ln2g_ref, ln2b_ref,
     y_ref, tailg_ref, tailv_ref, histg_ref, histv_ref, eg_ref, ev_ref, acc_ref) = refs
    tm = x1_ref.shape[0]
    d_ff = wug_ref.shape[1]
    taps = dwg_ref.shape[0]
    i = pl.program_id(1)

    @pl.when(i == 0)
    def _():
        if has_hist:
            histg_ref[...] = hg0_ref[...]
            histv_ref[...] = hv0_ref[...]
        else:
            histg_ref[...] = jnp.zeros(histg_ref.shape, F32)
            histv_ref[...] = jnp.zeros(histv_ref.shape, F32)

    x1 = x1_ref[...]
    x1b = x1.astype(BF16)
    base = pad - (taps - 1) * stride
    acc_ref[...] = jnp.zeros(acc_ref.shape, F32)

    def conv_half(e_ref, hist_ref, wu_ref, dw_ref, db_ref, c0):
        cols = slice(c0, c0 + chunk)
        e_ref[0:pad, :] = hist_ref[:, cols]
        e_ref[pad:pad + tm, :] = _dot(x1b, wu_ref[:, cols])
        hist_ref[:, cols] = e_ref[tm:tm + pad, :]
        h = jnp.broadcast_to(db_ref[:, cols], (tm, chunk))
        for k in range(taps):
            off = base + k * stride
            h = h + dw_ref[k:k + 1, cols] * e_ref[off:off + tm, :]
        return h

    for cidx in range(d_ff // chunk):
        c0 = cidx * chunk
        hg = conv_half(eg_ref, histg_ref, wug_ref, dwg_ref, dbg_ref, c0)
        hv = conv_half(ev_ref, histv_ref, wuv_ref, dwv_ref, dbv_ref, c0)
        act = (hg * jax.nn.sigmoid(hg) * hv).astype(BF16)
        acc_ref[...] += _dot(act, wdn_ref[c0:c0 + chunk, :])

    tailg_ref[0] = histg_ref[...]
    tailv_ref[0] = histv_ref[...]
    y_ref[...] = _layer_norm(alpha * x1 + acc_ref[...], ln2g_ref[...], ln2b_ref[...])


def _ffn(x1, hist_g, hist_v, w, *, n_seq, tm, stride, alpha, chunk):
    n, d = x1.shape
    d_ff = w["w_up_g"].shape[1]
    taps = w["ffn_dw_g"].shape[0]
    pad = -(-(taps - 1) * stride // SUBLANES) * SUBLANES
    tiles = n // n_seq // tm
    row = lambda s, i: (s * tiles + i, 0)
    weights = [w["w_up_g"], w["w_up_v"], w["ffn_dw_g"], w["ffn_dw_v"], w["ffn_db_g"], w["ffn_db_v"],
               w["w_down"], w["ln2_g"], w["ln2_b"]]
    has_hist = hist_g is not None
    in_specs = [pl.BlockSpec((tm, d), row)] + [_const_spec(a.shape) for a in weights]
    args = [x1, *weights]
    if has_hist:
        assert hist_g.shape == (pad, d_ff) and tiles == 1
        in_specs = [_const_spec(hist_g.shape), _const_spec(hist_v.shape)] + in_specs
        args = [hist_g, hist_v] + args
    else:
        assert tm >= pad
    tail_spec = pl.BlockSpec((1, pad, d_ff), lambda s, i: (s, 0, 0))
    return pl.pallas_call(
        functools.partial(_ffn_body, stride=stride, pad=pad, chunk=chunk, alpha=alpha, has_hist=has_hist),
        grid=(n_seq, tiles), in_specs=in_specs,
        out_specs=[pl.BlockSpec((tm, d), row), tail_spec, tail_spec],
        out_shape=[jax.ShapeDtypeStruct((n, d), F32),
                   jax.ShapeDtypeStruct((n_seq, pad, d_ff), F32),
                   jax.ShapeDtypeStruct((n_seq, pad, d_ff), F32)],
        scratch_shapes=[pltpu.VMEM((pad, d_ff), F32), pltpu.VMEM((pad, d_ff), F32),
                        pltpu.VMEM((pad + tm, chunk), F32), pltpu.VMEM((pad + tm, chunk), F32),
                        pltpu.VMEM((tm, d), F32)],
        compiler_params=_params("arbitrary", "arbitrary"), name="conv_ffn",
    )(*args)


def _rope_tables(positions, rope, n_heads):
    half = rope // 2
    inv = 1.0 / (ROPE_THETA ** (jnp.arange(0, rope, 2, dtype=F32) / rope))
    ang = positions.astype(F32)[:, None] * inv[None, :]
    cos, sin = jnp.cos(ang), jnp.sin(ang)
    cs = jnp.tile(jnp.concatenate([cos, cos], axis=-1), (1, n_heads))
    sn = jnp.tile(jnp.concatenate([-sin, sin], axis=-1), (1, n_heads))
    return cs, sn


def _prepare_weights(l, c_conv, w_in, b_in, conv_dw_w, conv_dw_b, conv_ln_g, conv_ln_b, w_conv_out, q_norm_g,
                     w_uq, kv_norm_g, w_uk, w_uv, w_mla_out, w_mix_out, ln1_g, ln1_b, w_up, ffn_dw_w,
                     ffn_dw_b, w_down, ln2_g, ln2_b, rope):
    kv_lora, n_heads, nope = w_uk.shape[1:]
    q_lora = q_norm_g.shape[1]
    d_ff = w_down.shape[1]
    s_q = 2 * c_conv
    s_kv = s_q + q_lora
    s_kr = s_kv + kv_lora
    s_gate = s_kr + rope
    wi, bi = w_in[l], b_in[l][None, :]
    uq = w_uq[l].reshape(q_lora, n_heads, nope + rope)
    row = lambda a: a[l][None, :]
    return {
        "w_glu": wi[:, :s_q].astype(BF16), "b_glu": bi[:, :s_q],
        "w_q": wi[:, s_q:s_kv].astype(BF16), "b_q": bi[:, s_q:s_kv],
        "w_kv": wi[:, s_kv:s_kr].astype(BF16), "b_kv": bi[:, s_kv:s_kr],
        "w_kr": wi[:, s_kr:s_gate].astype(BF16), "b_kr": bi[:, s_kr:s_gate],
        "w_gate": wi[:, s_gate:].astype(BF16), "b_gate": bi[:, s_gate:],
        "q_norm_g": row(q_norm_g), "kv_norm_g": row(kv_norm_g),
        "w_uq_nope": uq[:, :, :nope].reshape(q_lora, n_heads * nope).astype(BF16),
        "w_uq_rope": uq[:, :, nope:].reshape(q_lora, n_heads * rope).astype(BF16),
        "w_ukT": jnp.transpose(w_uk[l], (1, 2, 0)).astype(BF16),
        "w_uv": jnp.transpose(w_uv[l], (1, 0, 2)).astype(BF16),
        "conv_dw_w": conv_dw_w[l], "conv_dw_b": row(conv_dw_b),
        "conv_ln_g": row(conv_ln_g), "conv_ln_b": row(conv_ln_b),
        "w_conv_out": w_conv_out[l].astype(BF16), "w_mla_out": w_mla_out[l].astype(BF16),
        "w_mix_out": w_mix_out[l].astype(BF16), "ln1_g": row(ln1_g), "ln1_b": row(ln1_b),
        "w_up_g": w_up[l][:, :d_ff].astype(BF16), "w_up_v": w_up[l][:, d_ff:].astype(BF16),
        "ffn_dw_g": ffn_dw_w[l][:, :d_ff], "ffn_dw_v": ffn_dw_w[l][:, d_ff:],
        "ffn_db_g": ffn_dw_b[l][None, :d_ff], "ffn_db_v": ffn_dw_b[l][None, d_ff:],
        "w_down": w_down[l].astype(BF16), "ln2_g": row(ln2_g), "ln2_b": row(ln2_b),
    }


def _pick_tile(n, target):
    t = min(n, target)
    while n % t:
        t -= SUBLANES
    return t


def kernel(x_prompt, x_sample, cache_kv_latent, cache_k_rope, state_conv, state_ffn_conv, page_table, w_in, b_in, conv_dw_w, conv_dw_b, conv_ln_g, conv_ln_b, w_conv_out, q_norm_g, w_uq, kv_norm_g, w_uk, w_uv, w_mla_out, w_mix_out, ln1_g, ln1_b, w_up, ffn_dw_w, ffn_dw_b, w_down, ln2_g, ln2_b):
    depth = w_in.shape[0]
    bp, sp, d = x_prompt.shape
    bs, ts, _ = x_sample.shape
    c_conv = state_conv.shape[-1]
    conv_hist = state_conv.shape[2]
    ffn_hist = state_ffn_conv.shape[2]
    kv_lora, n_heads, nope = w_uk.shape[1:]
    rope = cache_k_rope.shape[-1]
    d_ff = w_down.shape[1]
    page = cache_kv_latent.shape[2]
    n_pages = page_table.shape[1]
    past_len = n_pages * page
    alpha = (2.0 * depth) ** 0.25
    scale = float(nope + rope) ** -0.5 * LOG2_E

    tm_p = _pick_tile(sp, 256)
    tq_attn = _pick_tile(sp, 256)
    tk_attn = 2 * tq_attn if sp % (2 * tq_attn) == 0 else tq_attn
    n_s = bs * ts
    tm_s = _pick_tile(n_s, 256)
    ffn_chunk = 256 if d_ff % 256 == 0 else 128
    pages_per_chunk = min(n_pages, 32)

    cs_p, sn_p = _rope_tables(jnp.arange(sp, dtype=jnp.int32), rope, n_heads)
    pos_s = past_len + jnp.arange(ts, dtype=jnp.int32)
    cs_s, sn_s = _rope_tables(jnp.repeat(pos_s, bs), rope, n_heads)

    h_p = x_prompt.reshape(bp * sp, d)
    h_s = jnp.transpose(x_sample, (1, 0, 2)).reshape(n_s, d)
    outs = [[] for _ in range(8)]
    for l in range(depth):
        w = _prepare_weights(l, c_conv, w_in, b_in, conv_dw_w, conv_dw_b, conv_ln_g, conv_ln_b, w_conv_out,
                             q_norm_g, w_uq, kv_norm_g, w_uk, w_uv, w_mla_out, w_mix_out, ln1_g, ln1_b,
                             w_up, ffn_dw_w, ffn_dw_b, w_down, ln2_g, ln2_b, rope)

        glu, ckv, kr, kcat, q, gates = _in_proj(h_p, cs_p, sn_p, w, tm=tm_p, scale=scale)
        v = _prompt_attention(q, kcat, w["w_uv"], batch=bp, seq=sp, tq=tq_attn, tk=tk_attn)
        x1 = _mix(glu, v, gates, h_p, None, w, n_seq=bp, tm=tm_p, stride=1, alpha=alpha)
        h_p, tail_g, tail_v = _ffn(x1, None, None, w, n_seq=bp, tm=tm_p, stride=1, alpha=alpha,
                                   chunk=ffn_chunk)
        outs[0].append(ckv.reshape(bp, sp, kv_lora))
        outs[1].append(kr.reshape(bp, sp, rope))
        outs[2].append(glu.reshape(bp, sp, c_conv)[:, sp - conv_hist:, :])
        outs[3].append(jnp.concatenate([tail_g, tail_v], axis=-1)[:, -ffn_hist:, :])

        glu, ckv, kr, kcat, q, gates = _in_proj(h_s, cs_s, sn_s, w, tm=tm_s, scale=scale)
        q_b = jnp.transpose(q.reshape(n_heads, ts, bs, kv_lora + rope), (2, 0, 1, 3))
        q_b = q_b.reshape(bs, n_heads * ts, kv_lora + rope)
        knew = jnp.transpose(kcat.reshape(ts, bs, kv_lora + rope), (1, 0, 2))
        knew = jnp.pad(knew, ((0, 0), (0, 128 - ts), (0, 0)))
        o_lat = _sample_attention(q_b, knew, cache_kv_latent[l], jnp.swapaxes(cache_k_rope[l], 1, 2), page_table,
                                  n_new=ts, pages_per_chunk=pages_per_chunk)
        o_lat = jnp.transpose(o_lat.reshape(bs, n_heads, ts, kv_lora), (1, 2, 0, 3))
        v = _uv_proj(o_lat.reshape(n_heads, n_s, kv_lora), w["w_uv"])
        conv_prev = jnp.transpose(state_conv[l], (1, 0, 2))
        x1 = _mix(glu, v, gates, h_s, conv_prev.reshape(conv_hist * bs, c_conv), w,
                  n_seq=1, tm=n_s, stride=bs, alpha=alpha)
        ffn_prev = jnp.transpose(state_ffn_conv[l], (1, 0, 2)).reshape(ffn_hist * bs, 2 * d_ff)
        h_s, tail_g, tail_v = _ffn(x1, ffn_prev[:, :d_ff], ffn_prev[:, d_ff:], w, n_seq=1, tm=n_s,
                                   stride=bs, alpha=alpha, chunk=ffn_chunk)
        to_batch_major = lambda a, t: jnp.transpose(a.reshape(t, bs, a.shape[-1]), (1, 0, 2))
        outs[4].append(to_batch_major(ckv, ts))
        outs[5].append(to_batch_major(kr, ts))
        conv_ext = jnp.concatenate([conv_prev, glu.reshape(ts, bs, c_conv)], axis=0)
        outs[6].append(jnp.transpose(conv_ext[-conv_hist:], (1, 0, 2)))
        ffn_tail = jnp.concatenate([tail_g[0], tail_v[0]], axis=-1)
        outs[7].append(to_batch_major(ffn_tail, ffn_hist))

    y_prompt = h_p.reshape(bp, sp, d)
    y_sample = jnp.transpose(h_s.reshape(ts, bs, d), (1, 0, 2))
    return (y_prompt, y_sample, *[jnp.stack(o, 0) for o in outs])
```

```python
import functools

import jax
import jax.numpy as jnp
from jax import lax
from jax.experimental import pallas as pl
from jax.experimental.pallas import tpu as pltpu

LN_EPS = 1e-5
RMS_EPS = 1e-6
ROPE_THETA = 10000.0
NEG_INF = float("-inf")

BF16 = jnp.bfloat16
F32 = jnp.float32

VMEM_LIMIT_BYTES = 56 * 1024 * 1024
SUBLANES = 8
LANES = 128
LOG2_E = 1.4426950408889634


def _dot(a, b):
    return jnp.dot(a, b, preferred_element_type=F32)


def _dot_nt(a, b):
    return lax.dot_general(a, b, (((1,), (1,)), ((), ())), preferred_element_type=F32)


def _layer_norm(x, g, b):
    mu = jnp.mean(x, axis=-1, keepdims=True)
    xc = x - mu
    var = jnp.mean(xc * xc, axis=-1, keepdims=True)
    return xc * lax.rsqrt(var + LN_EPS) * g + b


def _rms_norm(x, g):
    ms = jnp.mean(x * x, axis=-1, keepdims=True)
    return x * lax.rsqrt(ms + RMS_EPS) * g


def _params(*sem):
    return pltpu.CompilerParams(dimension_semantics=sem, vmem_limit_bytes=VMEM_LIMIT_BYTES)


def _const_spec(shape, single_buffer=True):
    nd = len(shape)
    mode = pl.Buffered(1) if single_buffer else None
    return pl.BlockSpec(shape, lambda *_: (0,) * nd, pipeline_mode=mode)


def _zero_after(v):
    bits = lax.bitcast_convert_type(v[0:1, 0:LANES], jnp.uint32)
    bits = lax.shift_right_logical(lax.shift_right_logical(bits, jnp.uint32(16)), jnp.uint32(16))
    return lax.bitcast_convert_type(bits, F32)


def _dwconv_block(ext_ref, w_ref, b_ref, r0, rc, c0, cw, base, stride, taps, after=None):
    cols = slice(c0, c0 + cw)
    groups = {}
    for k in range(taps):
        off = base + k * stride
        groups.setdefault(off % SUBLANES, []).append((k, off - off % SUBLANES))
    top = max(groups)
    cur = None
    for res in range(top, -1, -1):
        n_rows = rc + SUBLANES if (res > 0 and top > 0) else rc
        z = None
        for k, off in groups.get(res, []):
            w_k = w_ref[k:k + 1, cols]
            if after is not None and cur is None and z is None:
                w_k = w_k + after
            term = w_k * ext_ref[r0 + off:r0 + off + n_rows, cols]
            z = term if z is None else z + term
        if cur is not None:
            shifted = pltpu.roll(cur, cur.shape[0] - 1, 0)[:n_rows]
            z = shifted if z is None else z + shifted
        cur = z
    return cur + b_ref[:, cols]


def _conv_branch(i, glu, hist_ref, dww_ref, dwb_ref, lng_ref, lnb_ref, ext_ref, u_ref, uc_ref, *, stride):
    tm, c = glu.shape
    taps = dww_ref.shape[0]
    pad = ext_ref.shape[0] - tm
    row_chunk, col_chunk, ln_chunk = _pick_tile(tm, 64), _pick_tile(c, LANES), _pick_tile(tm, 32)

    @pl.when(i == 0)
    def _():
        if hist_ref is None:
            ext_ref[0:pad, :] = jnp.zeros((pad, c), F32)
        else:
            ext_ref[0:pad, :] = hist_ref[...]

    @pl.when(i > 0)
    def _():
        ext_ref[0:pad, :] = ext_ref[tm:tm + pad, :]

    ext_ref[pad:pad + tm, :] = glu

    base = pad - (taps - 1) * stride

    def conv_rows(r0, after=None):
        for c0 in range(0, c, col_chunk):
            u_ref[r0:r0 + row_chunk, c0:c0 + col_chunk] = _dwconv_block(
                ext_ref, dww_ref, dwb_ref, r0, row_chunk, c0, col_chunk, base, stride, taps, after=after)
        g = lng_ref[...]
        b = lnb_ref[...]
        for r1 in range(r0, r0 + row_chunk, ln_chunk):
            y = _layer_norm(u_ref[r1:r1 + ln_chunk, :], g, b)
            y = y * jax.nn.sigmoid(y)
            uc_ref[r1:r1 + ln_chunk, :] = y.astype(BF16)
        return _zero_after(y)

    return [functools.partial(conv_rows, r0) for r0 in range(0, tm, row_chunk)]


def _conv_pad(taps, stride):
    return -(-(taps - 1) * stride // SUBLANES) * SUBLANES


def _in_proj_body(*refs, scale, fuse_conv):
    (x_ref, cs_ref, sn_ref, wglu_ref, bglu_ref, wq_ref, bq_ref, wkv_ref, bkv_ref, wkr_ref, bkr_ref,
     wgt_ref, bgt_ref, qg_ref, kvg_ref, wuqn_ref, wuqr_ref, wuk_ref) = refs[:18]
    refs = refs[18:]
    if fuse_conv:
        conv_w_refs, refs = refs[:4], refs[4:]
    glu_ref, ckv_ref, kr_ref, kcat_ref, q_ref, gate_ref = refs[:6]
    n_heads, nope, kv_lora = wuk_ref.shape
    rope = kr_ref.shape[-1]
    half = rope // 2
    c_conv = glu_ref.shape[-1]

    xb = x_ref[...].astype(BF16)

    zg = _dot(xb, wglu_ref[...]) + bglu_ref[...]
    glu = zg[:, :c_conv] * jax.nn.sigmoid(zg[:, c_conv:])
    glu_ref[...] = glu
    conv_chunks = []
    if fuse_conv:
        uc_ref, ext_ref, u_ref = refs[6:]
        conv_chunks = _conv_branch(pl.program_id(1), glu, None, *conv_w_refs, ext_ref, u_ref, uc_ref, stride=1)

    prev = {"conv": None, "mxu": None}

    def after_conv(lhs):
        z = prev["conv"]
        return lhs if z is None else lhs + _lane_tile(z, lhs.shape[-1] // LANES).astype(lhs.dtype)

    def slot(stage):
        res = stage()
        if conv_chunks:
            conv_zero = conv_chunks.pop(0)(prev["mxu"])
            prev["mxu"] = _zero_after(res)
            prev["conv"] = conv_zero

    gw = gate_ref.shape[-1] // 4
    for j in range(4):
        def gate_stage(j=j):
            cols = slice(j * gw, (j + 1) * gw)
            g = jax.nn.sigmoid(_dot(after_conv(xb), wgt_ref[:, cols]) + bgt_ref[:, cols])
            gate_ref[:, cols] = g
            return g
        slot(gate_stage)

    cs = cs_ref[...]
    sn = sn_ref[...]
    carry = {}

    def kv_stage():
        lhs = after_conv(xb)
        c_kv = _rms_norm(_dot(lhs, wkv_ref[...]) + bkv_ref[...], kvg_ref[...])
        ckv_ref[...] = c_kv
        kcat_ref[:, :kv_lora] = c_kv.astype(BF16)
        zkr = _dot(lhs, wkr_ref[...]) + bkr_ref[...]
        zkr_rot = jnp.concatenate([zkr[:, half:], zkr[:, :half]], axis=-1)
        k_rope = zkr * cs[:, :rope] + zkr_rot * sn[:, :rope]
        kr_ref[...] = k_rope
        kcat_ref[:, kv_lora:] = k_rope.astype(BF16)
        carry["qn"] = _rms_norm(_dot(lhs, wq_ref[...]) + bq_ref[...], qg_ref[...]).astype(BF16)
        return c_kv
    slot(kv_stage)

    def q_up_stage():
        qn = after_conv(carry["qn"])
        q_nope = _dot(qn, wuqn_ref[...])
        q_rope = _dot(qn, wuqr_ref[...])
        width = q_rope.shape[-1]
        lane = lax.broadcasted_iota(jnp.int32, q_rope.shape, 1)
        first_half = (lane % rope) < half
        q_rot = jnp.where(first_half, pltpu.roll(q_rope, width - half, 1), pltpu.roll(q_rope, half, 1))
        reps = width // cs.shape[-1]
        carry["q_rope"] = (q_rope * _lane_tile(cs, reps) + q_rot * _lane_tile(sn, reps)) * scale
        carry["q_nope"] = q_nope
        return q_nope
    slot(q_up_stage)

    def heads_stage(h0, h1):
        q_nope, q_rope = carry["q_nope"], carry["q_rope"]
        for h in range(h0, h1):
            lhs = after_conv(q_nope[:, h * nope:(h + 1) * nope].astype(BF16))
            q_lat = _dot(lhs, wuk_ref[h]) * scale
            q_ref[h, :, :kv_lora] = q_lat.astype(BF16)
            q_ref[h, :, kv_lora:] = q_rope[:, h * rope:(h + 1) * rope].astype(BF16)
        return q_lat
    slot(functools.partial(heads_stage, 0, n_heads // 2))
    slot(functools.partial(heads_stage, n_heads // 2, n_heads))
    while conv_chunks:
        conv_chunks.pop(0)(None)


def _in_proj(x, cs, sn, w, *, n_seq, tm, scale, fuse_conv):
    n, d = x.shape
    n_heads, nope, kv_lora = w["w_ukT"].shape
    rope = w["w_kr"].shape[1]
    c_conv = w["w_glu"].shape[1] // 2
    n_tab = cs.shape[0] // tm
    tiles = n // n_seq // tm
    row = lambda s, i: (s * tiles + i, 0)
    tab = lambda s, i: ((s * tiles + i) % n_tab, 0)
    weights = [w["w_glu"], w["b_glu"], w["w_q"], w["b_q"], w["w_kv"], w["b_kv"], w["w_kr"], w["b_kr"],
               w["w_gate"], w["b_gate"], w["q_norm_g"], w["kv_norm_g"], w["w_uq_nope"], w["w_uq_rope"],
               w["w_ukT"]]
    out_shape = [jax.ShapeDtypeStruct((n, c_conv), F32),
                 jax.ShapeDtypeStruct((n, kv_lora), F32),
                 jax.ShapeDtypeStruct((n, rope), F32),
                 jax.ShapeDtypeStruct((n, kv_lora + rope), BF16),
                 jax.ShapeDtypeStruct((n_heads, n, kv_lora + rope), BF16),
                 jax.ShapeDtypeStruct((n, 2 * d), F32)]
    out_specs = [pl.BlockSpec((tm, c_conv), row),
                 pl.BlockSpec((tm, kv_lora), row),
                 pl.BlockSpec((tm, rope), row),
                 pl.BlockSpec((tm, kv_lora + rope), row),
                 pl.BlockSpec((n_heads, tm, kv_lora + rope), lambda s, i: (0, s * tiles + i, 0)),
                 pl.BlockSpec((tm, 2 * d), row)]
    scratch = []
    if fuse_conv:
        weights += [w["conv_dw_w"], w["conv_dw_b"], w["conv_ln_g"], w["conv_ln_b"]]
        pad = _conv_pad(w["conv_dw_w"].shape[0], 1)
        assert tm >= pad
        out_shape.append(jax.ShapeDtypeStruct((n, c_conv), BF16))
        out_specs.append(pl.BlockSpec((tm, c_conv), row))
        scratch = [pltpu.VMEM((pad + tm, c_conv), F32), pltpu.VMEM((tm, c_conv), F32)]
    in_specs = [pl.BlockSpec((tm, d), row),
                pl.BlockSpec((tm, cs.shape[1]), tab),
                pl.BlockSpec((tm, sn.shape[1]), tab)]
    in_specs += [_const_spec(a.shape) for a in weights]
    sem = ("arbitrary", "arbitrary") if fuse_conv else ("parallel", "parallel")
    return pl.pallas_call(
        functools.partial(_in_proj_body, scale=scale, fuse_conv=fuse_conv),
        grid=(n_seq, tiles), in_specs=in_specs, out_specs=out_specs, out_shape=out_shape,
        scratch_shapes=scratch, compiler_params=_params(*sem), name="in_proj",
    )(x, cs, sn, *weights)


def _lane_tile(x, n):
    return x if n == 1 else jnp.concatenate([x] * n, axis=-1)


def _lane_fold(p, lanes):
    out = p[:, :lanes]
    for c in range(1, p.shape[-1] // lanes):
        out = out + p[:, c * lanes:(c + 1) * lanes]
    return out


def _prompt_attn_body(q_ref, k_ref, wuv_ref, v_ref, s_ref, p_ref, m_ref, l_ref, a_ref, acc_ref, *, tq, tk, rb,
                      groups):
    n_heads, kv_lora, v_dim = wuv_ref.shape
    lanes = m_ref.shape[-1]
    i = pl.program_id(1)
    rows = n_heads * tq
    ratio = tk // tq
    q = q_ref[...].reshape(rows, q_ref.shape[-1])

    m_ref[...] = jnp.full(m_ref.shape, NEG_INF, F32)
    l_ref[...] = jnp.zeros(l_ref.shape, F32)
    acc_ref[...] = jnp.zeros(acc_ref.shape, F32)

    def softmax_block(sl, r0, width, masked):
        s = s_ref[sl, :width]
        if masked:
            q_pos = lax.broadcasted_iota(jnp.int32, (rb, width), 0) + r0 % tq
            k_pos = lax.broadcasted_iota(jnp.int32, (rb, width), 1)
            s = jnp.where(q_pos >= k_pos, s, NEG_INF)
        m_prev = m_ref[sl, :]
        m_new = jnp.maximum(m_prev, jnp.max(s, axis=-1, keepdims=True))
        alpha = jnp.exp2(m_prev - m_new)
        p = jnp.exp2(s - _lane_tile(m_new, width // lanes))
        l_ref[sl, :] = alpha * l_ref[sl, :] + _lane_fold(p, lanes)
        m_ref[sl, :] = m_new
        p_ref[sl, :width] = p.astype(BF16)
        a_ref[sl, :] = alpha

    def step(start, width, masked):
        k = k_ref[0, pl.ds(start, width), :]
        gr = rows // groups
        for g in range(groups):
            s_ref[g * gr:(g + 1) * gr, :width] = _dot_nt(q[g * gr:(g + 1) * gr], k)
        for g in range(groups):
            for r0 in range(g * gr, (g + 1) * gr, rb):
                softmax_block(slice(r0, r0 + rb), r0, width, masked)
            rows_g = slice(g * gr, (g + 1) * gr)
            acc_ref[rows_g, :] = (acc_ref[rows_g, :] * _lane_tile(a_ref[rows_g, :], kv_lora // lanes)
                                  + _dot(p_ref[rows_g, :width], k[:, :kv_lora]))

    def wide_step(j, carry):
        step(pl.multiple_of(j * tk, tk), tk, False)
        return carry

    n_wide = lax.div(i, ratio)
    lax.fori_loop(0, n_wide, wide_step, 0)
    if ratio > 1:
        def narrow_step(j, carry):
            step(pl.multiple_of(n_wide * tk + j * tq, tq), tq, False)
            return carry

        lax.fori_loop(0, lax.rem(i, ratio), narrow_step, 0)
    step(pl.multiple_of(i * tq, tq), tq, True)

    inv_l = 1.0 / jnp.sum(l_ref[...], axis=-1, keepdims=True)
    for h in range(n_heads):
        o_h = acc_ref[h * tq:(h + 1) * tq, :] * inv_l[h * tq:(h + 1) * tq, :]
        v_ref[:, h * v_dim:(h + 1) * v_dim] = _dot(o_h.astype(BF16), wuv_ref[h]).astype(v_ref.dtype)


def _prompt_attention(q, kcat, w_uv, *, batch, seq, tq, tk):
    n_heads, n, dk = q.shape
    kv_lora, v_dim = w_uv.shape[1], w_uv.shape[2]
    nq = seq // tq
    rows = n_heads * tq
    k3 = kcat.reshape(batch, seq, dk)
    return pl.pallas_call(
        functools.partial(_prompt_attn_body, tq=tq, tk=tk, rb=32, groups=8),
        grid=(batch, nq),
        in_specs=[pl.BlockSpec((n_heads, tq, dk), lambda b, i: (0, b * nq + i, 0)),
                  pl.BlockSpec((1, seq, dk), lambda b, i: (b, 0, 0)),
                  _const_spec(w_uv.shape)],
        out_specs=pl.BlockSpec((tq, n_heads * v_dim), lambda b, i: (b * nq + i, 0)),
        out_shape=jax.ShapeDtypeStruct((n, n_heads * v_dim), BF16),
        scratch_shapes=[pltpu.VMEM((rows, tk), F32),
                        pltpu.VMEM((rows, tk), BF16),
                        pltpu.VMEM((rows, LANES), F32),
                        pltpu.VMEM((rows, LANES), F32),
                        pltpu.VMEM((rows, LANES), F32),
                        pltpu.VMEM((rows, kv_lora), F32)],
        compiler_params=_params("parallel", "parallel"), name="prompt_attention",
    )(q, k3, w_uv)


def _sample_attn_body(pt_ref, q_ref, knew_ref, poolc_hbm, poolr_hbm, o_ref,
                      cbuf, rbuf, sem, s_ref, *, n_new, ck, piece):
    n_pages = pt_ref.shape[1]
    page = poolc_hbm.shape[1]
    kv_lora = poolc_hbm.shape[2]
    b = pl.program_id(0)
    n_seq = pl.num_programs(0)
    slot = jnp.bitwise_and(b, 1)

    def page_copies(b_, slot_):
        copies = []
        for p in range(n_pages):
            pid = pt_ref[b_, p]
            copies.append(pltpu.make_async_copy(
                poolc_hbm.at[pid], cbuf.at[slot_, pl.ds(p * page, page), :], sem.at[0, slot_]))
            copies.append(pltpu.make_async_copy(
                poolr_hbm.at[pid], rbuf.at[slot_, :, pl.ds(p * page, page)], sem.at[1, slot_]))
        return copies

    @pl.when(b == 0)
    def _():
        for cp in page_copies(0, 0):
            cp.start()

    @pl.when(b + 1 < n_seq)
    def _():
        for cp in page_copies(b + 1, 1 - slot):
            cp.start()

    for cp in page_copies(b, slot):
        cp.wait()

    q = q_ref[0]
    q_lat = q[:, :kv_lora]
    q_rope = q[:, kv_lora:]

    knew = knew_ref[0]
    s = _dot_nt(q, knew)
    t_q = lax.rem(lax.broadcasted_iota(jnp.int32, s.shape, 0), n_new)
    t_k = lax.broadcasted_iota(jnp.int32, s.shape, 1)
    s = jnp.where(t_k <= t_q, s, NEG_INF)
    m = jnp.max(s, axis=-1, keepdims=True)
    p = jnp.exp2(s - m)
    l = jnp.sum(p, axis=-1, keepdims=True)
    acc = _dot(p.astype(BF16), knew[:, :kv_lora])

    for k0 in range(0, n_pages * page, ck):
        latents = []
        for c0 in range(0, ck, piece):
            cb = cbuf[slot, k0 + c0:k0 + c0 + piece, :].astype(BF16)
            rb = rbuf[slot, :, k0 + c0:k0 + c0 + piece].astype(BF16)
            s_ref[:, c0:c0 + piece] = _dot_nt(q_lat, cb) + _dot(q_rope, rb)
            latents.append(cb)
        s = s_ref[...]
        m_new = jnp.maximum(m, jnp.max(s, axis=-1, keepdims=True))
        alpha = jnp.exp2(m - m_new)
        p = jnp.exp2(s - m_new)
        l = alpha * l + jnp.sum(p, axis=-1, keepdims=True)
        p = p.astype(BF16)
        pv = _dot(p[:, :piece], latents[0])
        for j in range(1, len(latents)):
            pv = pv + _dot(p[:, j * piece:(j + 1) * piece], latents[j])
        acc = alpha * acc + pv
        m = m_new
    o_ref[0] = acc / l


def _sample_attention(q, knew, pool_c, pool_rt, page_table, *, n_new):
    bd, rows, dk = q.shape
    n_pages = page_table.shape[1]
    page, kv_lora = pool_c.shape[1], pool_c.shape[2]
    rope = pool_rt.shape[1]
    past = n_pages * page
    ck = _pick_tile(past, 8192)
    piece = _pick_tile(ck, 1024)
    grid_spec = pltpu.PrefetchScalarGridSpec(
        num_scalar_prefetch=1,
        grid=(bd,),
        in_specs=[pl.BlockSpec((1, rows, dk), lambda b, pt: (b, 0, 0)),
                  pl.BlockSpec((1, knew.shape[1], dk), lambda b, pt: (b, 0, 0)),
                  pl.BlockSpec(memory_space=pl.ANY),
                  pl.BlockSpec(memory_space=pl.ANY)],
        out_specs=pl.BlockSpec((1, rows, kv_lora), lambda b, pt: (b, 0, 0)),
        scratch_shapes=[pltpu.VMEM((2, past, kv_lora), F32),
                        pltpu.VMEM((2, rope, past), F32),
                        pltpu.SemaphoreType.DMA((2, 2)),
                        pltpu.VMEM((rows, ck), F32)])
    return pl.pallas_call(
        functools.partial(_sample_attn_body, n_new=n_new, ck=ck, piece=piece),
        grid_spec=grid_spec,
        out_shape=jax.ShapeDtypeStruct((bd, rows, kv_lora), F32),
        compiler_params=_params("arbitrary"), name="sample_attention",
    )(page_table, q, knew, pool_c, pool_rt)


def _uv_proj_body(o_ref, wuv_ref, v_ref):
    n_heads, _, v_dim = wuv_ref.shape
    for h in range(n_heads):
        v_ref[:, h * v_dim:(h + 1) * v_dim] = _dot(o_ref[h].astype(BF16), wuv_ref[h]).astype(v_ref.dtype)


def _uv_proj(o_lat, w_uv):
    n_heads, n, kv_lora = o_lat.shape
    v_dim = w_uv.shape[2]
    return pl.pallas_call(
        _uv_proj_body,
        grid=(1,),
        in_specs=[_const_spec(o_lat.shape), _const_spec(w_uv.shape)],
        out_specs=_const_spec((n, n_heads * v_dim), single_buffer=False),
        out_shape=jax.ShapeDtypeStruct((n, n_heads * v_dim), BF16),
        compiler_params=_params("arbitrary"), name="uv_proj",
    )(o_lat, w_uv)


def _conv_hist_body(hist_ref, glu_ref, dww_ref, dwb_ref, lng_ref, lnb_ref, uc_ref, ext_ref, u_ref, *, stride):
    for conv_rows in _conv_branch(0, glu_ref[...], hist_ref, dww_ref, dwb_ref, lng_ref, lnb_ref, ext_ref, u_ref,
                                  uc_ref, stride=stride):
        conv_rows()


def _conv_hist(glu, hist, w, *, stride):
    n, c = glu.shape
    pad = _conv_pad(w["conv_dw_w"].shape[0], stride)
    assert hist.shape == (pad, c)
    args = [hist, glu, w["conv_dw_w"], w["conv_dw_b"], w["conv_ln_g"], w["conv_ln_b"]]
    return pl.pallas_call(
        functools.partial(_conv_hist_body, stride=stride),
        grid=(1,), in_specs=[_const_spec(a.shape) for a in args],
        out_specs=_const_spec((n, c), single_buffer=False),
        out_shape=jax.ShapeDtypeStruct((n, c), BF16),
        scratch_shapes=[pltpu.VMEM((pad + n, c), F32), pltpu.VMEM((n, c), F32)],
        compiler_params=_params("arbitrary"), name="conv_hist",
    )(*args)


def _mix_body(uc_ref, v_ref, gate_ref, x_ref, wco_ref, wmo_ref, wmix_ref, ln1g_ref, ln1b_ref, x1_ref, *, alpha):
    d = x_ref.shape[-1]
    y_conv = _dot(uc_ref[...], wco_ref[...])
    y_mla = _dot(v_ref[...], wmo_ref[...])
    gates = gate_ref[...]
    merged = gates[:, :d] * y_conv + gates[:, d:] * y_mla
    mix = _dot(merged.astype(BF16), wmix_ref[...])
    x1_ref[...] = _layer_norm(alpha * x_ref[...] + mix, ln1g_ref[...], ln1b_ref[...])


def _mix(uc, v, gates, x, w, *, tm, alpha):
    n, d = x.shape
    row = lambda i: (i, 0)
    weights = [w["w_conv_out"], w["w_mla_out"], w["w_mix_out"], w["ln1_g"], w["ln1_b"]]
    in_specs = [pl.BlockSpec((tm, uc.shape[1]), row), pl.BlockSpec((tm, v.shape[1]), row),
                pl.BlockSpec((tm, 2 * d), row), pl.BlockSpec((tm, d), row)]
    in_specs += [_const_spec(a.shape) for a in weights]
    return pl.pallas_call(
        functools.partial(_mix_body, alpha=alpha),
        grid=(n // tm,), in_specs=in_specs,
        out_specs=pl.BlockSpec((tm, d), row),
        out_shape=jax.ShapeDtypeStruct((n, d), F32),
        compiler_params=_params("parallel"), name="mix",
    )(uc, v, gates, x, *weights)


def _ffn_body(*refs, stride, pad, chunk, alpha, has_state):
    if has_state:
        state_ref, refs = refs[0], refs[1:]
    (x1_ref, wu_ref, dw_ref, db_ref, wdn_ref, ln2g_ref, ln2b_ref,
     y_ref, tail_ref, hist_ref, eg_ref, ev_ref, act_ref) = refs
    tm = x1_ref.shape[0]
    d_ff = wdn_ref.shape[0]
    width = 2 * d_ff
    taps = dw_ref.shape[0]
    i = pl.program_id(1)

    @pl.when(i == 0)
    def _():
        if has_state:
            for j in range(pad // stride):
                hist_ref[j * stride:(j + 1) * stride, :] = state_ref[:, j * width:(j + 1) * width]
        else:
            hist_ref[...] = jnp.zeros(hist_ref.shape, F32)

    x1 = x1_ref[...]
    x1b = x1.astype(BF16)
    base = pad - (taps - 1) * stride

    def conv_half(e_ref, c0):
        cols = slice(c0, c0 + chunk)
        e_ref[0:pad, :] = hist_ref[:, cols]
        e_ref[pad:pad + tm, :] = _dot(x1b, wu_ref[:, cols])
        hist_ref[:, cols] = e_ref[tm:tm + pad, :]
        h = jnp.broadcast_to(db_ref[:, cols], (tm, chunk))
        for k in range(taps):
            off = base + k * stride
            h = h + dw_ref[k:k + 1, cols] * e_ref[off:off + tm, :]
        return h

    for c0 in range(0, d_ff, chunk):
        hg = conv_half(eg_ref, c0)
        hv = conv_half(ev_ref, d_ff + c0)
        act_ref[:, c0:c0 + chunk] = (hg * jax.nn.sigmoid(hg) * hv).astype(BF16)

    if has_state:
        for j in range(pad // stride):
            tail_ref[:, j * width:(j + 1) * width] = hist_ref[j * stride:(j + 1) * stride, :]
    else:
        tail_ref[0] = hist_ref[...]
    f = _dot(act_ref[...], wdn_ref[...])
    y_ref[...] = _layer_norm(alpha * x1 + f, ln2g_ref[...], ln2b_ref[...])


def _ffn(x1, state, w, *, n_seq, tm, stride, alpha, chunk):
    n, d = x1.shape
    d_ff = w["w_down"].shape[0]
    width = 2 * d_ff
    taps = w["ffn_dw_w"].shape[0]
    pad = _conv_pad(taps, stride)
    tiles = n // n_seq // tm
    row = lambda s, i: (s * tiles + i, 0)
    weights = [w["w_up"], w["ffn_dw_w"], w["ffn_dw_b"], w["w_down"], w["ln2_g"], w["ln2_b"]]
    has_state = state is not None
    in_specs = [pl.BlockSpec((tm, d), row)] + [_const_spec(a.shape) for a in weights]
    args = [x1, *weights]
    if has_state:
        assert tiles == 1 and n_seq == 1 and pad == (taps - 1) * stride
        assert state.shape == (stride, (taps - 1) * width)
        in_specs = [_const_spec(state.shape)] + in_specs
        args = [state] + args
        tail_shape = jax.ShapeDtypeStruct(state.shape, F32)
        tail_spec = _const_spec(state.shape, single_buffer=False)
    else:
        assert tm >= pad
        tail_shape = jax.ShapeDtypeStruct((n_seq, pad, width), F32)
        tail_spec = pl.BlockSpec((1, pad, width), lambda s, i: (s, 0, 0))
    return pl.pallas_call(
        functools.partial(_ffn_body, stride=stride, pad=pad, chunk=chunk, alpha=alpha, has_state=has_state),
        grid=(n_seq, tiles), in_specs=in_specs,
        out_specs=[pl.BlockSpec((tm, d), row), tail_spec],
        out_shape=[jax.ShapeDtypeStruct((n, d), F32), tail_shape],
        scratch_shapes=[pltpu.VMEM((pad, width), F32),
                        pltpu.VMEM((pad + tm, chunk), F32), pltpu.VMEM((pad + tm, chunk), F32),
                        pltpu.VMEM((tm, d_ff), BF16)],
        compiler_params=_params("arbitrary", "arbitrary"), name="conv_ffn",
    )(*args)


def _rope_tables(positions, rope):
    inv = 1.0 / (ROPE_THETA ** (jnp.arange(0, rope, 2, dtype=F32) / rope))
    ang = positions.astype(F32)[:, None] * inv[None, :]
    cos, sin = jnp.cos(ang), jnp.sin(ang)
    reps = max(1, LANES // rope)
    cs = jnp.tile(jnp.concatenate([cos, cos], axis=-1), (1, reps))
    sn = jnp.tile(jnp.concatenate([-sin, sin], axis=-1), (1, reps))
    return cs, sn


def _prepare_weights(l, c_conv, w_in, b_in, conv_dw_w, conv_dw_b, conv_ln_g, conv_ln_b, w_conv_out, q_norm_g,
                     w_uq, kv_norm_g, w_uk, w_uv, w_mla_out, w_mix_out, ln1_g, ln1_b, w_up, ffn_dw_w,
                     ffn_dw_b, w_down, ln2_g, ln2_b, rope):
    kv_lora, n_heads, nope = w_uk.shape[1:]
    q_lora = q_norm_g.shape[1]
    d_ff = w_down.shape[1]
    s_q = 2 * c_conv
    s_kv = s_q + q_lora
    s_kr = s_kv + kv_lora
    s_gate = s_kr + rope
    wi, bi = w_in[l], b_in[l][None, :]
    uq = w_uq[l].reshape(q_lora, n_heads, nope + rope)
    row = lambda a: a[l][None, :]
    return {
        "w_glu": wi[:, :s_q].astype(BF16), "b_glu": bi[:, :s_q],
        "w_q": wi[:, s_q:s_kv].astype(BF16), "b_q": bi[:, s_q:s_kv],
        "w_kv": wi[:, s_kv:s_kr].astype(BF16), "b_kv": bi[:, s_kv:s_kr],
        "w_kr": wi[:, s_kr:s_gate].astype(BF16), "b_kr": bi[:, s_kr:s_gate],
        "w_gate": wi[:, s_gate:].astype(BF16), "b_gate": bi[:, s_gate:],
        "q_norm_g": row(q_norm_g), "kv_norm_g": row(kv_norm_g),
        "w_uq_nope": uq[:, :, :nope].reshape(q_lora, n_heads * nope).astype(BF16),
        "w_uq_rope": uq[:, :, nope:].reshape(q_lora, n_heads * rope).astype(BF16),
        "w_ukT": jnp.transpose(w_uk[l], (1, 2, 0)).astype(BF16),
        "w_uv": jnp.transpose(w_uv[l], (1, 0, 2)).astype(BF16),
        "conv_dw_w": conv_dw_w[l], "conv_dw_b": row(conv_dw_b),
        "conv_ln_g": row(conv_ln_g), "conv_ln_b": row(conv_ln_b),
        "w_conv_out": w_conv_out[l].astype(BF16), "w_mla_out": w_mla_out[l].astype(BF16),
        "w_mix_out": w_mix_out[l].astype(BF16), "ln1_g": row(ln1_g), "ln1_b": row(ln1_b),
        "w_up": w_up[l].astype(BF16), "ffn_dw_w": ffn_dw_w[l], "ffn_dw_b": row(ffn_dw_b),
        "w_down": w_down[l].astype(BF16), "ln2_g": row(ln2_g), "ln2_b": row(ln2_b),
    }


def _pick_tile(n, target):
    t = min(n, target)
    while n % t:
        t -= SUBLANES
    return t


def kernel(x_prompt, x_sample, cache_kv_latent, cache_k_rope, state_conv, state_ffn_conv, page_table, w_in, b_in, conv_dw_w, conv_dw_b, conv_ln_g, conv_ln_b, w_conv_out, q_norm_g, w_uq, kv_norm_g, w_uk, w_uv, w_mla_out, w_mix_out, ln1_g, ln1_b, w_up, ffn_dw_w, ffn_dw_b, w_down, ln2_g, ln2_b):
    depth = w_in.shape[0]
    bp, sp, d = x_prompt.shape
    bs, ts, _ = x_sample.shape
    c_conv = state_conv.shape[-1]
    conv_hist = state_conv.shape[2]
    ffn_hist = state_ffn_conv.shape[2]
    kv_lora, n_heads, nope = w_uk.shape[1:]
    rope = cache_k_rope.shape[-1]
    d_ff = w_down.shape[1]
    page = cache_kv_latent.shape[2]
    n_pages = page_table.shape[1]
    past_len = n_pages * page
    alpha = (2.0 * depth) ** 0.25
    scale = float(nope + rope) ** -0.5 * LOG2_E

    tm_p = _pick_tile(sp, 512)
    tm_ffn = _pick_tile(sp, 512)
    tq_attn = _pick_tile(sp, 256)
    tk_attn = 2 * tq_attn if sp % (2 * tq_attn) == 0 else tq_attn
    n_s = bs * ts
    tm_s = _pick_tile(n_s, 256)
    ffn_chunk = 256 if d_ff % 256 == 0 else 128

    cs_p, sn_p = _rope_tables(jnp.arange(sp, dtype=jnp.int32), rope)
    pos_s = past_len + jnp.arange(ts, dtype=jnp.int32)
    cs_s, sn_s = _rope_tables(jnp.repeat(pos_s, bs), rope)

    h_p = x_prompt.reshape(bp * sp, d)
    h_s = jnp.transpose(x_sample, (1, 0, 2)).reshape(n_s, d)
    outs = [[] for _ in range(8)]
    for l in range(depth):
        w = _prepare_weights(l, c_conv, w_in, b_in, conv_dw_w, conv_dw_b, conv_ln_g, conv_ln_b, w_conv_out,
                             q_norm_g, w_uq, kv_norm_g, w_uk, w_uv, w_mla_out, w_mix_out, ln1_g, ln1_b,
                             w_up, ffn_dw_w, ffn_dw_b, w_down, ln2_g, ln2_b, rope)

        glu, ckv, kr, kcat, q, gates, uc = _in_proj(h_p, cs_p, sn_p, w, n_seq=bp, tm=tm_p, scale=scale,
                                                    fuse_conv=True)
        v = _prompt_attention(q, kcat, w["w_uv"], batch=bp, seq=sp, tq=tq_attn, tk=tk_attn)
        x1 = _mix(uc, v, gates, h_p, w, tm=tm_p, alpha=alpha)
        h_p, tail = _ffn(x1, None, w, n_seq=bp, tm=tm_ffn, stride=1, alpha=alpha, chunk=ffn_chunk)
        outs[0].append(ckv.reshape(bp, sp, kv_lora))
        outs[1].append(kr.reshape(bp, sp, rope))
        outs[2].append(glu.reshape(bp, sp, c_conv)[:, sp - conv_hist:, :])
        outs[3].append(tail[:, -ffn_hist:, :])

        glu, ckv, kr, kcat, q, gates = _in_proj(h_s, cs_s, sn_s, w, n_seq=1, tm=tm_s, scale=scale,
                                                fuse_conv=False)
        q_b = jnp.transpose(q.reshape(n_heads, ts, bs, kv_lora + rope), (2, 0, 1, 3))
        q_b = q_b.reshape(bs, n_heads * ts, kv_lora + rope)
        knew = jnp.transpose(kcat.reshape(ts, bs, kv_lora + rope), (1, 0, 2))
        knew = jnp.pad(knew, ((0, 0), (0, 128 - ts), (0, 0)))
        o_lat = _sample_attention(q_b, knew, cache_kv_latent[l], jnp.swapaxes(cache_k_rope[l], 1, 2), page_table,
                                  n_new=ts)
        o_lat = jnp.transpose(o_lat.reshape(bs, n_heads, ts, kv_lora), (1, 2, 0, 3))
        v = _uv_proj(o_lat.reshape(n_heads, n_s, kv_lora), w["w_uv"])
        conv_prev = jnp.transpose(state_conv[l], (1, 0, 2))
        uc = _conv_hist(glu, conv_prev.reshape(conv_hist * bs, c_conv), w, stride=bs)
        x1 = _mix(uc, v, gates, h_s, w, tm=tm_s, alpha=alpha)
        h_s, tail = _ffn(x1, state_ffn_conv[l].reshape(bs, ffn_hist * 2 * d_ff), w, n_seq=1, tm=n_s,
                         stride=bs, alpha=alpha, chunk=ffn_chunk)
        to_batch_major = lambda a, t: jnp.transpose(a.reshape(t, bs, a.shape[-1]), (1, 0, 2))
        outs[4].append(to_batch_major(ckv, ts))
        outs[5].append(to_batch_major(kr, ts))
        conv_ext = jnp.concatenate([conv_prev, glu.reshape(ts, bs, c_conv)], axis=0)
        outs[6].append(jnp.transpose(conv_ext[-conv_hist:], (1, 0, 2)))
        outs[7].append(tail.reshape(bs, ffn_hist, 2 * d_ff))

    y_prompt = h_p.reshape(bp, sp, d)
    y_sample = jnp.transpose(h_s.reshape(ts, bs, d), (1, 0, 2))
    return (y_prompt, y_sample, *[jnp.stack(o, 0) for o in outs])
```

```python
import functools

import jax
import jax.numpy as jnp
from jax import lax
from jax.experimental import pallas as pl
from jax.experimental.pallas import tpu as pltpu

LN_EPS = 1e-5
RMS_EPS = 1e-6
ROPE_THETA = 10000.0
NEG_INF = float("-inf")

BF16 = jnp.bfloat16
F32 = jnp.float32

VMEM_LIMIT_BYTES = 56 * 1024 * 1024
SUBLANES = 8
LANES = 128
LOG2_E = 1.4426950408889634


def _dot(a, b):
    return jnp.dot(a, b, preferred_element_type=F32)


def _dot_nt(a, b):
    return lax.dot_general(a, b, (((1,), (1,)), ((), ())), preferred_element_type=F32)


def _layer_norm(x, g, b):
    mu = jnp.mean(x, axis=-1, keepdims=True)
    xc = x - mu
    var = jnp.mean(xc * xc, axis=-1, keepdims=True)
    return xc * lax.rsqrt(var + LN_EPS) * g + b


def _rms_norm(x, g):
    ms = jnp.mean(x * x, axis=-1, keepdims=True)
    return x * lax.rsqrt(ms + RMS_EPS) * g


def _params(*sem):
    return pltpu.CompilerParams(dimension_semantics=sem, vmem_limit_bytes=VMEM_LIMIT_BYTES)


def _const_spec(shape, single_buffer=True):
    nd = len(shape)
    mode = pl.Buffered(1) if single_buffer else None
    return pl.BlockSpec(shape, lambda *_: (0,) * nd, pipeline_mode=mode)


def _zero_after(v):
    bits = lax.bitcast_convert_type(v[0:1, 0:LANES], jnp.uint32)
    bits = lax.shift_right_logical(lax.shift_right_logical(bits, jnp.uint32(16)), jnp.uint32(16))
    return lax.bitcast_convert_type(bits, F32)


def _dwconv_block(ext_ref, w_ref, b_ref, r0, rc, c0, cw, base, stride, taps, after=None):
    cols = slice(c0, c0 + cw)
    groups = {}
    for k in range(taps):
        off = base + k * stride
        groups.setdefault(off % SUBLANES, []).append((k, off - off % SUBLANES))
    top = max(groups)
    cur = None
    for res in range(top, -1, -1):
        n_rows = rc + SUBLANES if (res > 0 and top > 0) else rc
        z = None
        for k, off in groups.get(res, []):
            w_k = w_ref[k:k + 1, cols]
            if after is not None and cur is None and z is None:
                w_k = w_k + after
            term = w_k * ext_ref[r0 + off:r0 + off + n_rows, cols]
            z = term if z is None else z + term
        if cur is not None:
            shifted = pltpu.roll(cur, cur.shape[0] - 1, 0)[:n_rows]
            z = shifted if z is None else z + shifted
        cur = z
    return cur + b_ref[:, cols]


def _conv_branch(i, glu, hist_ref, dww_ref, dwb_ref, lng_ref, lnb_ref, ext_ref, u_ref, uc_ref, *, stride):
    tm, c = glu.shape
    taps = dww_ref.shape[0]
    pad = ext_ref.shape[0] - tm
    row_chunk, col_chunk, ln_chunk = _pick_tile(tm, 64), _pick_tile(c, LANES), _pick_tile(tm, 32)

    @pl.when(i == 0)
    def _():
        if hist_ref is None:
            ext_ref[0:pad, :] = jnp.zeros((pad, c), F32)
        else:
            ext_ref[0:pad, :] = hist_ref[...]

    @pl.when(i > 0)
    def _():
        ext_ref[0:pad, :] = ext_ref[tm:tm + pad, :]

    ext_ref[pad:pad + tm, :] = glu

    base = pad - (taps - 1) * stride

    def conv_rows(r0, after=None):
        for c0 in range(0, c, col_chunk):
            u_ref[r0:r0 + row_chunk, c0:c0 + col_chunk] = _dwconv_block(
                ext_ref, dww_ref, dwb_ref, r0, row_chunk, c0, col_chunk, base, stride, taps, after=after)
        g = lng_ref[...]
        b = lnb_ref[...]
        for r1 in range(r0, r0 + row_chunk, ln_chunk):
            y = _layer_norm(u_ref[r1:r1 + ln_chunk, :], g, b)
            y = y * jax.nn.sigmoid(y)
            uc_ref[r1:r1 + ln_chunk, :] = y.astype(BF16)
        return _zero_after(y)

    return [functools.partial(conv_rows, r0) for r0 in range(0, tm, row_chunk)]


def _conv_pad(taps, stride):
    return -(-(taps - 1) * stride // SUBLANES) * SUBLANES


def _in_proj_body(*refs, scale, fuse_conv):
    (x_ref, cs_ref, sn_ref, wglu_ref, bglu_ref, wq_ref, bq_ref, wkv_ref, bkv_ref, wkr_ref, bkr_ref,
     wgt_ref, bgt_ref, qg_ref, kvg_ref, wuqn_ref, wuqr_ref, wuk_ref) = refs[:18]
    refs = refs[18:]
    if fuse_conv:
        conv_w_refs, refs = refs[:4], refs[4:]
    glu_ref, ckv_ref, kr_ref, kcat_ref, q_ref, gate_ref = refs[:6]
    n_heads, nope, kv_lora = wuk_ref.shape
    rope = kr_ref.shape[-1]
    half = rope // 2
    c_conv = glu_ref.shape[-1]

    xb = x_ref[...].astype(BF16)

    zg = _dot(xb, wglu_ref[...]) + bglu_ref[...]
    glu = zg[:, :c_conv] * jax.nn.sigmoid(zg[:, c_conv:])
    glu_ref[...] = glu
    conv_chunks = []
    if fuse_conv:
        uc_ref, ext_ref, u_ref = refs[6:]
        conv_chunks = _conv_branch(pl.program_id(1), glu, None, *conv_w_refs, ext_ref, u_ref, uc_ref, stride=1)

    prev = {"conv": None, "mxu": None}

    def after_conv(lhs):
        z = prev["conv"]
        return lhs if z is None else lhs + _lane_tile(z, lhs.shape[-1] // LANES).astype(lhs.dtype)

    def slot(stage):
        res = stage()
        if conv_chunks:
            conv_zero = conv_chunks.pop(0)(prev["mxu"])
            prev["mxu"] = _zero_after(res)
            prev["conv"] = conv_zero

    gw = gate_ref.shape[-1] // 4
    for j in range(4):
        def gate_stage(j=j):
            cols = slice(j * gw, (j + 1) * gw)
            g = jax.nn.sigmoid(_dot(after_conv(xb), wgt_ref[:, cols]) + bgt_ref[:, cols])
            gate_ref[:, cols] = g
            return g
        slot(gate_stage)

    cs = cs_ref[...]
    sn = sn_ref[...]
    carry = {}

    def kv_stage():
        lhs = after_conv(xb)
        c_kv = _rms_norm(_dot(lhs, wkv_ref[...]) + bkv_ref[...], kvg_ref[...])
        ckv_ref[...] = c_kv
        kcat_ref[:, :kv_lora] = c_kv.astype(BF16)
        zkr = _dot(lhs, wkr_ref[...]) + bkr_ref[...]
        zkr_rot = jnp.concatenate([zkr[:, half:], zkr[:, :half]], axis=-1)
        k_rope = zkr * cs[:, :rope] + zkr_rot * sn[:, :rope]
        kr_ref[...] = k_rope
        kcat_ref[:, kv_lora:] = k_rope.astype(BF16)
        carry["qn"] = _rms_norm(_dot(lhs, wq_ref[...]) + bq_ref[...], qg_ref[...]).astype(BF16)
        return c_kv
    slot(kv_stage)

    def q_up_stage():
        qn = after_conv(carry["qn"])
        q_nope = _dot(qn, wuqn_ref[...])
        q_rope = _dot(qn, wuqr_ref[...])
        width = q_rope.shape[-1]
        lane = lax.broadcasted_iota(jnp.int32, q_rope.shape, 1)
        first_half = (lane % rope) < half
        q_rot = jnp.where(first_half, pltpu.roll(q_rope, width - half, 1), pltpu.roll(q_rope, half, 1))
        reps = width // cs.shape[-1]
        carry["q_rope"] = (q_rope * _lane_tile(cs, reps) + q_rot * _lane_tile(sn, reps)) * scale
        carry["q_nope"] = q_nope
        return q_nope
    slot(q_up_stage)

    def heads_stage(h0, h1):
        q_nope, q_rope = carry["q_nope"], carry["q_rope"]
        for h in range(h0, h1):
            lhs = after_conv(q_nope[:, h * nope:(h + 1) * nope].astype(BF16))
            q_lat = _dot(lhs, wuk_ref[h]) * scale
            q_ref[h, :, :kv_lora] = q_lat.astype(BF16)
            q_ref[h, :, kv_lora:] = q_rope[:, h * rope:(h + 1) * rope].astype(BF16)
        return q_lat
    slot(functools.partial(heads_stage, 0, n_heads // 2))
    slot(functools.partial(heads_stage, n_heads // 2, n_heads))
    while conv_chunks:
        conv_chunks.pop(0)(None)


def _in_proj(x, cs, sn, w, *, n_seq, tm, scale, fuse_conv):
    n, d = x.shape
    n_heads, nope, kv_lora = w["w_ukT"].shape
    rope = w["w_kr"].shape[1]
    c_conv = w["w_glu"].shape[1] // 2
    n_tab = cs.shape[0] // tm
    tiles = n // n_seq // tm
    row = lambda s, i: (s * tiles + i, 0)
    tab = lambda s, i: ((s * tiles + i) % n_tab, 0)
    weights = [w["w_glu"], w["b_glu"], w["w_q"], w["b_q"], w["w_kv"], w["b_kv"], w["w_kr"], w["b_kr"],
               w["w_gate"], w["b_gate"], w["q_norm_g"], w["kv_norm_g"], w["w_uq_nope"], w["w_uq_rope"],
               w["w_ukT"]]
    out_shape = [jax.ShapeDtypeStruct((n, c_conv), F32),
                 jax.ShapeDtypeStruct((n, kv_lora), F32),
                 jax.ShapeDtypeStruct((n, rope), F32),
                 jax.ShapeDtypeStruct((n, kv_lora + rope), BF16),
                 jax.ShapeDtypeStruct((n_heads, n, kv_lora + rope), BF16),
                 jax.ShapeDtypeStruct((n, 2 * d), F32)]
    out_specs = [pl.BlockSpec((tm, c_conv), row),
                 pl.BlockSpec((tm, kv_lora), row),
                 pl.BlockSpec((tm, rope), row),
                 pl.BlockSpec((tm, kv_lora + rope), row),
                 pl.BlockSpec((n_heads, tm, kv_lora + rope), lambda s, i: (0, s * tiles + i, 0)),
                 pl.BlockSpec((tm, 2 * d), row)]
    scratch = []
    if fuse_conv:
        weights += [w["conv_dw_w"], w["conv_dw_b"], w["conv_ln_g"], w["conv_ln_b"]]
        pad = _conv_pad(w["conv_dw_w"].shape[0], 1)
        assert tm >= pad
        out_shape.append(jax.ShapeDtypeStruct((n, c_conv), BF16))
        out_specs.append(pl.BlockSpec((tm, c_conv), row))
        scratch = [pltpu.VMEM((pad + tm, c_conv), F32), pltpu.VMEM((tm, c_conv), F32)]
    in_specs = [pl.BlockSpec((tm, d), row),
                pl.BlockSpec((tm, cs.shape[1]), tab),
                pl.BlockSpec((tm, sn.shape[1]), tab)]
    in_specs += [_const_spec(a.shape) for a in weights]
    sem = ("arbitrary", "arbitrary") if fuse_conv else ("parallel", "parallel")
    return pl.pallas_call(
        functools.partial(_in_proj_body, scale=scale, fuse_conv=fuse_conv),
        grid=(n_seq, tiles), in_specs=in_specs, out_specs=out_specs, out_shape=out_shape,
        scratch_shapes=scratch, compiler_params=_params(*sem), name="in_proj",
    )(x, cs, sn, *weights)


def _lane_tile(x, n):
    return x if n == 1 else jnp.concatenate([x] * n, axis=-1)


def _lane_fold(p, lanes):
    out = p[:, :lanes]
    for c in range(1, p.shape[-1] // lanes):
        out = out + p[:, c * lanes:(c + 1) * lanes]
    return out


def _prompt_attn_body(q_ref, k_ref, wuv_ref, v_ref, s_ref, p_ref, m_ref, l_ref, a_ref, acc_ref, *, tq, tk, rb,
                      groups):
    n_heads, kv_lora, v_dim = wuv_ref.shape
    lanes = m_ref.shape[-1]
    i = pl.program_id(1)
    rows = n_heads * tq
    ratio = tk // tq
    q = q_ref[...].reshape(rows, q_ref.shape[-1])

    m_ref[...] = jnp.full(m_ref.shape, NEG_INF, F32)
    l_ref[...] = jnp.zeros(l_ref.shape, F32)
    acc_ref[...] = jnp.zeros(acc_ref.shape, F32)

    def softmax_block(sl, r0, width, masked):
        s = s_ref[sl, :width]
        if masked:
            q_pos = lax.broadcasted_iota(jnp.int32, (rb, width), 0) + r0 % tq
            k_pos = lax.broadcasted_iota(jnp.int32, (rb, width), 1)
            s = jnp.where(q_pos >= k_pos, s, NEG_INF)
        m_prev = m_ref[sl, :]
        m_new = jnp.maximum(m_prev, jnp.max(s, axis=-1, keepdims=True))
        alpha = jnp.exp2(m_prev - m_new)
        p = jnp.exp2(s - _lane_tile(m_new, width // lanes))
        l_ref[sl, :] = alpha * l_ref[sl, :] + _lane_fold(p, lanes)
        m_ref[sl, :] = m_new
        p_ref[sl, :width] = p.astype(BF16)
        a_ref[sl, :] = alpha

    def step(start, width, masked):
        k = k_ref[0, pl.ds(start, width), :]
        gr = rows // groups
        for g in range(groups):
            s_ref[g * gr:(g + 1) * gr, :width] = _dot_nt(q[g * gr:(g + 1) * gr], k)
        for g in range(groups):
            for r0 in range(g * gr, (g + 1) * gr, rb):
                softmax_block(slice(r0, r0 + rb), r0, width, masked)
            rows_g = slice(g * gr, (g + 1) * gr)
            acc_ref[rows_g, :] = (acc_ref[rows_g, :] * _lane_tile(a_ref[rows_g, :], kv_lora // lanes)
                                  + _dot(p_ref[rows_g, :width], k[:, :kv_lora]))

    def wide_step(j, carry):
        step(pl.multiple_of(j * tk, tk), tk, False)
        return carry

    n_wide = lax.div(i, ratio)
    lax.fori_loop(0, n_wide, wide_step, 0)
    if ratio > 1:
        def narrow_step(j, carry):
            step(pl.multiple_of(n_wide * tk + j * tq, tq), tq, False)
            return carry

        lax.fori_loop(0, lax.rem(i, ratio), narrow_step, 0)
    step(pl.multiple_of(i * tq, tq), tq, True)

    inv_l = 1.0 / jnp.sum(l_ref[...], axis=-1, keepdims=True)
    for h in range(n_heads):
        o_h = acc_ref[h * tq:(h + 1) * tq, :] * inv_l[h * tq:(h + 1) * tq, :]
        v_ref[:, h * v_dim:(h + 1) * v_dim] = _dot(o_h.astype(BF16), wuv_ref[h]).astype(v_ref.dtype)


def _prompt_attention(q, kcat, w_uv, *, batch, seq, tq, tk):
    n_heads, n, dk = q.shape
    kv_lora, v_dim = w_uv.shape[1], w_uv.shape[2]
    nq = seq // tq
    rows = n_heads * tq
    k3 = kcat.reshape(batch, seq, dk)
    return pl.pallas_call(
        functools.partial(_prompt_attn_body, tq=tq, tk=tk, rb=32, groups=4),
        grid=(batch, nq),
        in_specs=[pl.BlockSpec((n_heads, tq, dk), lambda b, i: (0, b * nq + i, 0)),
                  pl.BlockSpec((1, seq, dk), lambda b, i: (b, 0, 0)),
                  _const_spec(w_uv.shape)],
        out_specs=pl.BlockSpec((tq, n_heads * v_dim), lambda b, i: (b * nq + i, 0)),
        out_shape=jax.ShapeDtypeStruct((n, n_heads * v_dim), BF16),
        scratch_shapes=[pltpu.VMEM((rows, tk), F32),
                        pltpu.VMEM((rows, tk), BF16),
                        pltpu.VMEM((rows, LANES), F32),
                        pltpu.VMEM((rows, LANES), F32),
                        pltpu.VMEM((rows, LANES), F32),
                        pltpu.VMEM((rows, kv_lora), F32)],
        compiler_params=_params("parallel", "parallel"), name="prompt_attention",
    )(q, k3, w_uv)


def _sample_attn_body(pt_ref, q_ref, knew_ref, poolc_hbm, poolr_hbm, o_ref,
                      cbuf, rbuf, sem, s_ref, *, n_new, ck, piece):
    n_pages = pt_ref.shape[1]
    page = poolc_hbm.shape[1]
    kv_lora = poolc_hbm.shape[2]
    b = pl.program_id(0)
    n_seq = pl.num_programs(0)
    slot = jnp.bitwise_and(b, 1)

    def page_copies(b_, slot_):
        copies = []
        for p in range(n_pages):
            pid = pt_ref[b_, p]
            copies.append(pltpu.make_async_copy(
                poolc_hbm.at[pid], cbuf.at[slot_, pl.ds(p * page, page), :], sem.at[0, slot_]))
            copies.append(pltpu.make_async_copy(
                poolr_hbm.at[pid], rbuf.at[slot_, :, pl.ds(p * page, page)], sem.at[1, slot_]))
        return copies

    @pl.when(b == 0)
    def _():
        for cp in page_copies(0, 0):
            cp.start()

    @pl.when(b + 1 < n_seq)
    def _():
        for cp in page_copies(b + 1, 1 - slot):
            cp.start()

    for cp in page_copies(b, slot):
        cp.wait()

    q = q_ref[0]
    q_lat = q[:, :kv_lora]
    q_rope = q[:, kv_lora:]

    knew = knew_ref[0]
    s = _dot_nt(q, knew)
    t_q = lax.rem(lax.broadcasted_iota(jnp.int32, s.shape, 0), n_new)
    t_k = lax.broadcasted_iota(jnp.int32, s.shape, 1)
    s = jnp.where(t_k <= t_q, s, NEG_INF)
    m = jnp.max(s, axis=-1, keepdims=True)
    p = jnp.exp2(s - m)
    l = jnp.sum(p, axis=-1, keepdims=True)
    acc = _dot(p.astype(BF16), knew[:, :kv_lora])

    for k0 in range(0, n_pages * page, ck):
        latents = []
        for c0 in range(0, ck, piece):
            cb = cbuf[slot, k0 + c0:k0 + c0 + piece, :].astype(BF16)
            rb = rbuf[slot, :, k0 + c0:k0 + c0 + piece].astype(BF16)
            s_ref[:, c0:c0 + piece] = _dot_nt(q_lat, cb) + _dot(q_rope, rb)
            latents.append(cb)
        s = s_ref[...]
        m_new = jnp.maximum(m, jnp.max(s, axis=-1, keepdims=True))
        alpha = jnp.exp2(m - m_new)
        p = jnp.exp2(s - m_new)
        l = alpha * l + jnp.sum(p, axis=-1, keepdims=True)
        p = p.astype(BF16)
        pv = _dot(p[:, :piece], latents[0])
        for j in range(1, len(latents)):
            pv = pv + _dot(p[:, j * piece:(j + 1) * piece], latents[j])
        acc = alpha * acc + pv
        m = m_new
    o_ref[0] = acc / l


def _sample_attention(q, knew, pool_c, pool_rt, page_table, *, n_new):
    bd, rows, dk = q.shape
    n_pages = page_table.shape[1]
    page, kv_lora = pool_c.shape[1], pool_c.shape[2]
    rope = pool_rt.shape[1]
    past = n_pages * page
    ck = _pick_tile(past, 8192)
    piece = _pick_tile(ck, 1024)
    grid_spec = pltpu.PrefetchScalarGridSpec(
        num_scalar_prefetch=1,
        grid=(bd,),
        in_specs=[pl.BlockSpec((1, rows, dk), lambda b, pt: (b, 0, 0)),
                  pl.BlockSpec((1, knew.shape[1], dk), lambda b, pt: (b, 0, 0)),
                  pl.BlockSpec(memory_space=pl.ANY),
                  pl.BlockSpec(memory_space=pl.ANY)],
        out_specs=pl.BlockSpec((1, rows, kv_lora), lambda b, pt: (b, 0, 0)),
        scratch_shapes=[pltpu.VMEM((2, past, kv_lora), F32),
                        pltpu.VMEM((2, rope, past), F32),
                        pltpu.SemaphoreType.DMA((2, 2)),
                        pltpu.VMEM((rows, ck), F32)])
    return pl.pallas_call(
        functools.partial(_sample_attn_body, n_new=n_new, ck=ck, piece=piece),
        grid_spec=grid_spec,
        out_shape=jax.ShapeDtypeStruct((bd, rows, kv_lora), F32),
        compiler_params=_params("arbitrary"), name="sample_attention",
    )(page_table, q, knew, pool_c, pool_rt)


def _uv_proj_body(o_ref, wuv_ref, v_ref):
    n_heads, _, v_dim = wuv_ref.shape
    for h in range(n_heads):
        v_ref[:, h * v_dim:(h + 1) * v_dim] = _dot(o_ref[h].astype(BF16), wuv_ref[h]).astype(v_ref.dtype)


def _uv_proj(o_lat, w_uv):
    n_heads, n, kv_lora = o_lat.shape
    v_dim = w_uv.shape[2]
    return pl.pallas_call(
        _uv_proj_body,
        grid=(1,),
        in_specs=[_const_spec(o_lat.shape), _const_spec(w_uv.shape)],
        out_specs=_const_spec((n, n_heads * v_dim), single_buffer=False),
        out_shape=jax.ShapeDtypeStruct((n, n_heads * v_dim), BF16),
        compiler_params=_params("arbitrary"), name="uv_proj",
    )(o_lat, w_uv)


def _conv_hist_body(hist_ref, glu_ref, dww_ref, dwb_ref, lng_ref, lnb_ref, uc_ref, ext_ref, u_ref, *, stride):
    for conv_rows in _conv_branch(0, glu_ref[...], hist_ref, dww_ref, dwb_ref, lng_ref, lnb_ref, ext_ref, u_ref,
                                  uc_ref, stride=stride):
        conv_rows()


def _conv_hist(glu, hist, w, *, stride):
    n, c = glu.shape
    pad = _conv_pad(w["conv_dw_w"].shape[0], stride)
    assert hist.shape == (pad, c)
    args = [hist, glu, w["conv_dw_w"], w["conv_dw_b"], w["conv_ln_g"], w["conv_ln_b"]]
    return pl.pallas_call(
        functools.partial(_conv_hist_body, stride=stride),
        grid=(1,), in_specs=[_const_spec(a.shape) for a in args],
        out_specs=_const_spec((n, c), single_buffer=False),
        out_shape=jax.ShapeDtypeStruct((n, c), BF16),
        scratch_shapes=[pltpu.VMEM((pad + n, c), F32), pltpu.VMEM((n, c), F32)],
        compiler_params=_params("arbitrary"), name="conv_hist",
    )(*args)


def _mix_body(uc_ref, v_ref, gate_ref, x_ref, wco_ref, wmo_ref, wmix_ref, ln1g_ref, ln1b_ref, x1_ref, *, alpha):
    d = x_ref.shape[-1]
    y_conv = _dot(uc_ref[...], wco_ref[...])
    y_mla = _dot(v_ref[...], wmo_ref[...])
    gates = gate_ref[...]
    merged = gates[:, :d] * y_conv + gates[:, d:] * y_mla
    mix = _dot(merged.astype(BF16), wmix_ref[...])
    x1_ref[...] = _layer_norm(alpha * x_ref[...] + mix, ln1g_ref[...], ln1b_ref[...])


def _mix(uc, v, gates, x, w, *, tm, alpha):
    n, d = x.shape
    row = lambda i: (i, 0)
    weights = [w["w_conv_out"], w["w_mla_out"], w["w_mix_out"], w["ln1_g"], w["ln1_b"]]
    in_specs = [pl.BlockSpec((tm, uc.shape[1]), row), pl.BlockSpec((tm, v.shape[1]), row),
                pl.BlockSpec((tm, 2 * d), row), pl.BlockSpec((tm, d), row)]
    in_specs += [_const_spec(a.shape) for a in weights]
    return pl.pallas_call(
        functools.partial(_mix_body, alpha=alpha),
        grid=(n // tm,), in_specs=in_specs,
        out_specs=pl.BlockSpec((tm, d), row),
        out_shape=jax.ShapeDtypeStruct((n, d), F32),
        compiler_params=_params("parallel"), name="mix",
    )(uc, v, gates, x, *weights)


def _ffn_body(*refs, stride, pad, chunk, alpha, has_state):
    if has_state:
        state_ref, refs = refs[0], refs[1:]
    (x1_ref, wu_ref, dw_ref, db_ref, wdn_ref, ln2g_ref, ln2b_ref,
     y_ref, tail_ref, hist_ref, eg_ref, ev_ref, act_ref) = refs
    tm = x1_ref.shape[0]
    d_ff = wdn_ref.shape[0]
    width = 2 * d_ff
    taps = dw_ref.shape[0]
    i = pl.program_id(1)

    @pl.when(i == 0)
    def _():
        if has_state:
            for j in range(pad // stride):
                hist_ref[j * stride:(j + 1) * stride, :] = state_ref[:, j * width:(j + 1) * width]
        else:
            hist_ref[...] = jnp.zeros(hist_ref.shape, F32)

    x1 = x1_ref[...]
    x1b = x1.astype(BF16)
    base = pad - (taps - 1) * stride

    def conv_half(e_ref, c0):
        cols = slice(c0, c0 + chunk)
        e_ref[0:pad, :] = hist_ref[:, cols]
        e_ref[pad:pad + tm, :] = _dot(x1b, wu_ref[:, cols])
        hist_ref[:, cols] = e_ref[tm:tm + pad, :]
        h = jnp.broadcast_to(db_ref[:, cols], (tm, chunk))
        for k in range(taps):
            off = base + k * stride
            h = h + dw_ref[k:k + 1, cols] * e_ref[off:off + tm, :]
        return h

    for c0 in range(0, d_ff, chunk):
        hg = conv_half(eg_ref, c0)
        hv = conv_half(ev_ref, d_ff + c0)
        act_ref[:, c0:c0 + chunk] = (hg * jax.nn.sigmoid(hg) * hv).astype(BF16)

    if has_state:
        for j in range(pad // stride):
            tail_ref[:, j * width:(j + 1) * width] = hist_ref[j * stride:(j + 1) * stride, :]
    else:
        tail_ref[0] = hist_ref[...]
    f = _dot(act_ref[...], wdn_ref[...])
    y_ref[...] = _layer_norm(alpha * x1 + f, ln2g_ref[...], ln2b_ref[...])


def _ffn(x1, state, w, *, n_seq, tm, stride, alpha, chunk):
    n, d = x1.shape
    d_ff = w["w_down"].shape[0]
    width = 2 * d_ff
    taps = w["ffn_dw_w"].shape[0]
    pad = _conv_pad(taps, stride)
    tiles = n // n_seq // tm
    row = lambda s, i: (s * tiles + i, 0)
    weights = [w["w_up"], w["ffn_dw_w"], w["ffn_dw_b"], w["w_down"], w["ln2_g"], w["ln2_b"]]
    has_state = state is not None
    in_specs = [pl.BlockSpec((tm, d), row)] + [_const_spec(a.shape) for a in weights]
    args = [x1, *weights]
    if has_state:
        assert tiles == 1 and n_seq == 1 and pad == (taps - 1) * stride
        assert state.shape == (stride, (taps - 1) * width)
        in_specs = [_const_spec(state.shape)] + in_specs
        args = [state] + args
        tail_shape = jax.ShapeDtypeStruct(state.shape, F32)
        tail_spec = _const_spec(state.shape, single_buffer=False)
    else:
        assert tm >= pad
        tail_shape = jax.ShapeDtypeStruct((n_seq, pad, width), F32)
        tail_spec = pl.BlockSpec((1, pad, width), lambda s, i: (s, 0, 0))
    return pl.pallas_call(
        functools.partial(_ffn_body, stride=stride, pad=pad, chunk=chunk, alpha=alpha, has_state=has_state),
        grid=(n_seq, tiles), in_specs=in_specs,
        out_specs=[pl.BlockSpec((tm, d), row), tail_spec],
        out_shape=[jax.ShapeDtypeStruct((n, d), F32), tail_shape],
        scratch_shapes=[pltpu.VMEM((pad, width), F32),
                        pltpu.VMEM((pad + tm, chunk), F32), pltpu.VMEM((pad + tm, chunk), F32),
                        pltpu.VMEM((tm, d_ff), BF16)],
        compiler_params=_params("arbitrary", "arbitrary"), name="conv_ffn",
    )(*args)


def _rope_tables(positions, rope):
    inv = 1.0 / (ROPE_THETA ** (jnp.arange(0, rope, 2, dtype=F32) / rope))
    ang = positions.astype(F32)[:, None] * inv[None, :]
    cos, sin = jnp.cos(ang), jnp.sin(ang)
    reps = max(1, LANES // rope)
    cs = jnp.tile(jnp.concatenate([cos, cos], axis=-1), (1, reps))
    sn = jnp.tile(jnp.concatenate([-sin, sin], axis=-1), (1, reps))
    return cs, sn


def _prepare_weights(l, c_conv, w_in, b_in, conv_dw_w, conv_dw_b, conv_ln_g, conv_ln_b, w_conv_out, q_norm_g,
                     w_uq, kv_norm_g, w_uk, w_uv, w_mla_out, w_mix_out, ln1_g, ln1_b, w_up, ffn_dw_w,
                     ffn_dw_b, w_down, ln2_g, ln2_b, rope):
    kv_lora, n_heads, nope = w_uk.shape[1:]
    q_lora = q_norm_g.shape[1]
    d_ff = w_down.shape[1]
    s_q = 2 * c_conv
    s_kv = s_q + q_lora
    s_kr = s_kv + kv_lora
    s_gate = s_kr + rope
    wi, bi = w_in[l], b_in[l][None, :]
    uq = w_uq[l].reshape(q_lora, n_heads, nope + rope)
    row = lambda a: a[l][None, :]
    return {
        "w_glu": wi[:, :s_q].astype(BF16), "b_glu": bi[:, :s_q],
        "w_q": wi[:, s_q:s_kv].astype(BF16), "b_q": bi[:, s_q:s_kv],
        "w_kv": wi[:, s_kv:s_kr].astype(BF16), "b_kv": bi[:, s_kv:s_kr],
        "w_kr": wi[:, s_kr:s_gate].astype(BF16), "b_kr": bi[:, s_kr:s_gate],
        "w_gate": wi[:, s_gate:].astype(BF16), "b_gate": bi[:, s_gate:],
        "q_norm_g": row(q_norm_g), "kv_norm_g": row(kv_norm_g),
        "w_uq_nope": uq[:, :, :nope].reshape(q_lora, n_heads * nope).astype(BF16),
        "w_uq_rope": uq[:, :, nope:].reshape(q_lora, n_heads * rope).astype(BF16),
        "w_ukT": jnp.transpose(w_uk[l], (1, 2, 0)).astype(BF16),
        "w_uv": jnp.transpose(w_uv[l], (1, 0, 2)).astype(BF16),
        "conv_dw_w": conv_dw_w[l], "conv_dw_b": row(conv_dw_b),
        "conv_ln_g": row(conv_ln_g), "conv_ln_b": row(conv_ln_b),
        "w_conv_out": w_conv_out[l].astype(BF16), "w_mla_out": w_mla_out[l].astype(BF16),
        "w_mix_out": w_mix_out[l].astype(BF16), "ln1_g": row(ln1_g), "ln1_b": row(ln1_b),
        "w_up": w_up[l].astype(BF16), "ffn_dw_w": ffn_dw_w[l], "ffn_dw_b": row(ffn_dw_b),
        "w_down": w_down[l].astype(BF16), "ln2_g": row(ln2_g), "ln2_b": row(ln2_b),
    }


def _pick_tile(n, target):
    t = min(n, target)
    while n % t:
        t -= SUBLANES
    return t


def kernel(x_prompt, x_sample, cache_kv_latent, cache_k_rope, state_conv, state_ffn_conv, page_table, w_in, b_in, conv_dw_w, conv_dw_b, conv_ln_g, conv_ln_b, w_conv_out, q_norm_g, w_uq, kv_norm_g, w_uk, w_uv, w_mla_out, w_mix_out, ln1_g, ln1_b, w_up, ffn_dw_w, ffn_dw_b, w_down, ln2_g, ln2_b):
    depth = w_in.shape[0]
    bp, sp, d = x_prompt.shape
    bs, ts, _ = x_sample.shape
    c_conv = state_conv.shape[-1]
    conv_hist = state_conv.shape[2]
    ffn_hist = state_ffn_conv.shape[2]
    kv_lora, n_heads, nope = w_uk.shape[1:]
    rope = cache_k_rope.shape[-1]
    d_ff = w_down.shape[1]
    page = cache_kv_latent.shape[2]
    n_pages = page_table.shape[1]
    past_len = n_pages * page
    alpha = (2.0 * depth) ** 0.25
    scale = float(nope + rope) ** -0.5 * LOG2_E

    tm_p = _pick_tile(sp, 512)
    tm_ffn = _pick_tile(sp, 512)
    tq_attn = _pick_tile(sp, 256)
    tk_attn = 2 * tq_attn if sp % (2 * tq_attn) == 0 else tq_attn
    n_s = bs * ts
    tm_s = _pick_tile(n_s, 256)
    ffn_chunk = 256 if d_ff % 256 == 0 else 128

    cs_p, sn_p = _rope_tables(jnp.arange(sp, dtype=jnp.int32), rope)
    pos_s = past_len + jnp.arange(ts, dtype=jnp.int32)
    cs_s, sn_s = _rope_tables(jnp.repeat(pos_s, bs), rope)

    h_p = x_prompt.reshape(bp * sp, d)
    h_s = jnp.transpose(x_sample, (1, 0, 2)).reshape(n_s, d)
    outs = [[] for _ in range(8)]
    for l in range(depth):
        w = _prepare_weights(l, c_conv, w_in, b_in, conv_dw_w, conv_dw_b, conv_ln_g, conv_ln_b, w_conv_out,
                             q_norm_g, w_uq, kv_norm_g, w_uk, w_uv, w_mla_out, w_mix_out, ln1_g, ln1_b,
                             w_up, ffn_dw_w, ffn_dw_b, w_down, ln2_g, ln2_b, rope)

        glu, ckv, kr, kcat, q, gates, uc = _in_proj(h_p, cs_p, sn_p, w, n_seq=bp, tm=tm_p, scale=scale,
                                                    fuse_conv=True)
        v = _prompt_attention(q, kcat, w["w_uv"], batch=bp, seq=sp, tq=tq_attn, tk=tk_attn)
        x1 = _mix(uc, v, gates, h_p, w, tm=tm_p, alpha=alpha)
        h_p, tail = _ffn(x1, None, w, n_seq=bp, tm=tm_ffn, stride=1, alpha=alpha, chunk=ffn_chunk)
        outs[0].append(ckv.reshape(bp, sp, kv_lora))
        outs[1].append(kr.reshape(bp, sp, rope))
        outs[2].append(glu.reshape(bp, sp, c_conv)[:, sp - conv_hist:, :])
        outs[3].append(tail[:, -ffn_hist:, :])

        glu, ckv, kr, kcat, q, gates = _in_proj(h_s, cs_s, sn_s, w, n_seq=1, tm=tm_s, scale=scale,
                                                fuse_conv=False)
        q_b = jnp.transpose(q.reshape(n_heads, ts, bs, kv_lora + rope), (2, 0, 1, 3))
        q_b = q_b.reshape(bs, n_heads * ts, kv_lora + rope)
        knew = jnp.transpose(kcat.reshape(ts, bs, kv_lora + rope), (1, 0, 2))
        knew = jnp.pad(knew, ((0, 0), (0, 128 - ts), (0, 0)))
        o_lat = _sample_attention(q_b, knew, cache_kv_latent[l], jnp.swapaxes(cache_k_rope[l], 1, 2), page_table,
                                  n_new=ts)
        o_lat = jnp.transpose(o_lat.reshape(bs, n_heads, ts, kv_lora), (1, 2, 0, 3))
        v = _uv_proj(o_lat.reshape(n_heads, n_s, kv_lora), w["w_uv"])
        conv_prev = jnp.transpose(state_conv[l], (1, 0, 2))
        uc = _conv_hist(glu, conv_prev.reshape(conv_hist * bs, c_conv), w, stride=bs)
        x1 = _mix(uc, v, gates, h_s, w, tm=tm_s, alpha=alpha)
        h_s, tail = _ffn(x1, state_ffn_conv[l].reshape(bs, ffn_hist * 2 * d_ff), w, n_seq=1, tm=n_s,
                         stride=bs, alpha=alpha, chunk=ffn_chunk)
        to_batch_major = lambda a, t: jnp.transpose(a.reshape(t, bs, a.shape[-1]), (1, 0, 2))
        outs[4].append(to_batch_major(ckv, ts))
        outs[5].append(to_batch_major(kr, ts))
        conv_ext = jnp.concatenate([conv_prev, glu.reshape(ts, bs, c_conv)], axis=0)
        outs[6].append(jnp.transpose(conv_ext[-conv_hist:], (1, 0, 2)))
        outs[7].append(tail.reshape(bs, ffn_hist, 2 * d_ff))

    y_prompt = h_p.reshape(bp, sp, d)
    y_sample = jnp.transpose(h_s.reshape(ts, bs, d), (1, 0, 2))
    return (y_prompt, y_sample, *[jnp.stack(o, 0) for o in outs])
```

```python
import functools

import jax
import jax.numpy as jnp
from jax import lax
from jax.experimental import pallas as pl
from jax.experimental.pallas import tpu as pltpu

LN_EPS = 1e-5
RMS_EPS = 1e-6
ROPE_THETA = 10000.0
NEG_INF = float("-inf")

BF16 = jnp.bfloat16
F32 = jnp.float32

VMEM_LIMIT_BYTES = 56 * 1024 * 1024
SUBLANES = 8
LANES = 128
LOG2_E = 1.4426950408889634


def _dot(a, b):
    return jnp.dot(a, b, preferred_element_type=F32)


def _dot_nt(a, b):
    return lax.dot_general(a, b, (((1,), (1,)), ((), ())), preferred_element_type=F32)


def _layer_norm(x, g, b):
    mu = jnp.mean(x, axis=-1, keepdims=True)
    xc = x - mu
    var = jnp.mean(xc * xc, axis=-1, keepdims=True)
    return xc * lax.rsqrt(var + LN_EPS) * g + b


def _rms_norm(x, g):
    ms = jnp.mean(x * x, axis=-1, keepdims=True)
    return x * lax.rsqrt(ms + RMS_EPS) * g


def _params(*sem):
    return pltpu.CompilerParams(dimension_semantics=sem, vmem_limit_bytes=VMEM_LIMIT_BYTES)


def _const_spec(shape, single_buffer=True):
    nd = len(shape)
    mode = pl.Buffered(1) if single_buffer else None
    return pl.BlockSpec(shape, lambda *_: (0,) * nd, pipeline_mode=mode)


def _zero_after(v):
    bits = lax.bitcast_convert_type(v[0:1, 0:LANES], jnp.uint32)
    bits = lax.shift_right_logical(lax.shift_right_logical(bits, jnp.uint32(16)), jnp.uint32(16))
    return lax.bitcast_convert_type(bits, F32)


def _dwconv_block(ext_ref, w_ref, b_ref, r0, rc, c0, cw, base, stride, taps, after=None):
    cols = slice(c0, c0 + cw)
    groups = {}
    for k in range(taps):
        off = base + k * stride
        groups.setdefault(off % SUBLANES, []).append((k, off - off % SUBLANES))
    top = max(groups)
    cur = None
    for res in range(top, -1, -1):
        n_rows = rc + SUBLANES if (res > 0 and top > 0) else rc
        z = None
        for k, off in groups.get(res, []):
            w_k = w_ref[k:k + 1, cols]
            if after is not None and cur is None and z is None:
                w_k = w_k + after
            term = w_k * ext_ref[r0 + off:r0 + off + n_rows, cols]
            z = term if z is None else z + term
        if cur is not None:
            shifted = pltpu.roll(cur, cur.shape[0] - 1, 0)[:n_rows]
            z = shifted if z is None else z + shifted
        cur = z
    return cur + b_ref[:, cols]


def _conv_branch(i, glu, hist_ref, dww_ref, dwb_ref, lng_ref, lnb_ref, ext_ref, u_ref, uc_ref, *, stride):
    tm, c = u_ref.shape
    taps = dww_ref.shape[0]
    pad = ext_ref.shape[0] - tm
    row_chunk, col_chunk, ln_chunk = _pick_tile(tm, 64), _pick_tile(c, LANES), _pick_tile(tm, 32)

    @pl.when(i == 0)
    def _():
        if hist_ref is None:
            ext_ref[0:pad, :] = jnp.zeros((pad, c), F32)
        else:
            ext_ref[0:pad, :] = hist_ref[...]

    @pl.when(i > 0)
    def _():
        ext_ref[0:pad, :] = ext_ref[tm:tm + pad, :]

    if glu is not None:
        ext_ref[pad:pad + tm, :] = glu

    base = pad - (taps - 1) * stride

    def conv_rows(r0, after=None):
        for c0 in range(0, c, col_chunk):
            u_ref[r0:r0 + row_chunk, c0:c0 + col_chunk] = _dwconv_block(
                ext_ref, dww_ref, dwb_ref, r0, row_chunk, c0, col_chunk, base, stride, taps, after=after)
        g = lng_ref[...]
        b = lnb_ref[...]
        for r1 in range(r0, r0 + row_chunk, ln_chunk):
            y = _layer_norm(u_ref[r1:r1 + ln_chunk, :], g, b)
            y = y * jax.nn.sigmoid(y)
            uc_ref[r1:r1 + ln_chunk, :] = y.astype(BF16)
        return _zero_after(y)

    return [functools.partial(conv_rows, r0) for r0 in range(0, tm, row_chunk)]


def _conv_pad(taps, stride):
    return -(-(taps - 1) * stride // SUBLANES) * SUBLANES


def _in_proj_body(*refs, scale, fuse_conv):
    (x_ref, cs_ref, sn_ref, wglu_ref, bglu_ref, wq_ref, bq_ref, wkv_ref, bkv_ref, wkr_ref, bkr_ref,
     wgt_ref, bgt_ref, qg_ref, kvg_ref, wuqn_ref, wuqr_ref, wuk_ref) = refs[:18]
    refs = refs[18:]
    if fuse_conv:
        conv_w_refs, refs = refs[:4], refs[4:]
    glu_ref, ckv_ref, kr_ref, kcat_ref, q_ref, gate_ref = refs[:6]
    n_heads, nope, kv_lora = wuk_ref.shape
    rope = kr_ref.shape[-1]
    half = rope // 2
    c_conv = glu_ref.shape[-1]

    tm = x_ref.shape[0]
    conv_chunks = []
    if fuse_conv:
        uc_ref, ext_ref, u_ref = refs[6:]
        conv_chunks = _conv_branch(pl.program_id(1), None, None, *conv_w_refs, ext_ref, u_ref, uc_ref, stride=1)

    xb = x_ref[...].astype(BF16)

    def glu_rows(r0, r1, lhs):
        zg = _dot(lhs, wglu_ref[...]) + bglu_ref[...]
        glu = zg[:, :c_conv] * jax.nn.sigmoid(zg[:, c_conv:])
        glu_ref[r0:r1, :] = glu
        if fuse_conv:
            pad = ext_ref.shape[0] - tm
            ext_ref[pad + r0:pad + r1, :] = glu
        return glu

    split = tm // 2 if fuse_conv and tm % (2 * 64) == 0 else tm
    glu_rows(0, split, xb[:split])

    prev = {"conv": None, "mxu": None}

    def after_conv(lhs):
        z = prev["conv"]
        return lhs if z is None else lhs + _lane_tile(z, lhs.shape[-1] // LANES).astype(lhs.dtype)

    def slot(stage):
        res = stage()
        if conv_chunks:
            conv_zero = conv_chunks.pop(0)(prev["mxu"])
            prev["mxu"] = _zero_after(res)
            prev["conv"] = conv_zero

    if split < tm:
        slot(lambda: glu_rows(split, tm, after_conv(xb[split:])))

    gw = gate_ref.shape[-1] // 4
    for j in range(4):
        def gate_stage(j=j):
            cols = slice(j * gw, (j + 1) * gw)
            g = jax.nn.sigmoid(_dot(after_conv(xb), wgt_ref[:, cols]) + bgt_ref[:, cols])
            gate_ref[:, cols] = g
            return g
        slot(gate_stage)

    cs = cs_ref[...]
    sn = sn_ref[...]
    carry = {}

    def kv_stage():
        lhs = after_conv(xb)
        c_kv = _rms_norm(_dot(lhs, wkv_ref[...]) + bkv_ref[...], kvg_ref[...])
        ckv_ref[...] = c_kv
        kcat_ref[:, :kv_lora] = c_kv.astype(BF16)
        zkr = _dot(lhs, wkr_ref[...]) + bkr_ref[...]
        zkr_rot = jnp.concatenate([zkr[:, half:], zkr[:, :half]], axis=-1)
        k_rope = zkr * cs[:, :rope] + zkr_rot * sn[:, :rope]
        kr_ref[...] = k_rope
        kcat_ref[:, kv_lora:] = k_rope.astype(BF16)
        carry["qn"] = _rms_norm(_dot(lhs, wq_ref[...]) + bq_ref[...], qg_ref[...]).astype(BF16)
        return c_kv
    slot(kv_stage)

    def q_up_stage():
        qn = after_conv(carry["qn"])
        q_nope = _dot(qn, wuqn_ref[...])
        q_rope = _dot(qn, wuqr_ref[...])
        width = q_rope.shape[-1]
        lane = lax.broadcasted_iota(jnp.int32, q_rope.shape, 1)
        first_half = (lane % rope) < half
        q_rot = jnp.where(first_half, pltpu.roll(q_rope, width - half, 1), pltpu.roll(q_rope, half, 1))
        reps = width // cs.shape[-1]
        carry["q_rope"] = (q_rope * _lane_tile(cs, reps) + q_rot * _lane_tile(sn, reps)) * scale
        carry["q_nope"] = q_nope
        return q_nope
    slot(q_up_stage)

    def heads_stage(h0, h1):
        q_nope, q_rope = carry["q_nope"], carry["q_rope"]
        for h in range(h0, h1):
            lhs = after_conv(q_nope[:, h * nope:(h + 1) * nope].astype(BF16))
            q_lat = _dot(lhs, wuk_ref[h]) * scale
            q_ref[h, :, :kv_lora] = q_lat.astype(BF16)
            q_ref[h, :, kv_lora:] = q_rope[:, h * rope:(h + 1) * rope].astype(BF16)
        return q_lat
    slot(functools.partial(heads_stage, 0, n_heads // 2))
    slot(functools.partial(heads_stage, n_heads // 2, n_heads))
    while conv_chunks:
        conv_chunks.pop(0)(None)


def _in_proj(x, cs, sn, w, *, n_seq, tm, scale, fuse_conv):
    n, d = x.shape
    n_heads, nope, kv_lora = w["w_ukT"].shape
    rope = w["w_kr"].shape[1]
    c_conv = w["w_glu"].shape[1] // 2
    n_tab = cs.shape[0] // tm
    tiles = n // n_seq // tm
    row = lambda s, i: (s * tiles + i, 0)
    tab = lambda s, i: ((s * tiles + i) % n_tab, 0)
    weights = [w["w_glu"], w["b_glu"], w["w_q"], w["b_q"], w["w_kv"], w["b_kv"], w["w_kr"], w["b_kr"],
               w["w_gate"], w["b_gate"], w["q_norm_g"], w["kv_norm_g"], w["w_uq_nope"], w["w_uq_rope"],
               w["w_ukT"]]
    out_shape = [jax.ShapeDtypeStruct((n, c_conv), F32),
                 jax.ShapeDtypeStruct((n, kv_lora), F32),
                 jax.ShapeDtypeStruct((n, rope), F32),
                 jax.ShapeDtypeStruct((n, kv_lora + rope), BF16),
                 jax.ShapeDtypeStruct((n_heads, n, kv_lora + rope), BF16),
                 jax.ShapeDtypeStruct((n, 2 * d), F32)]
    out_specs = [pl.BlockSpec((tm, c_conv), row),
                 pl.BlockSpec((tm, kv_lora), row),
                 pl.BlockSpec((tm, rope), row),
                 pl.BlockSpec((tm, kv_lora + rope), row),
                 pl.BlockSpec((n_heads, tm, kv_lora + rope), lambda s, i: (0, s * tiles + i, 0)),
                 pl.BlockSpec((tm, 2 * d), row)]
    scratch = []
    if fuse_conv:
        weights += [w["conv_dw_w"], w["conv_dw_b"], w["conv_ln_g"], w["conv_ln_b"]]
        pad = _conv_pad(w["conv_dw_w"].shape[0], 1)
        assert tm >= pad
        out_shape.append(jax.ShapeDtypeStruct((n, c_conv), BF16))
        out_specs.append(pl.BlockSpec((tm, c_conv), row))
        scratch = [pltpu.VMEM((pad + tm, c_conv), F32), pltpu.VMEM((tm, c_conv), F32)]
    in_specs = [pl.BlockSpec((tm, d), row),
                pl.BlockSpec((tm, cs.shape[1]), tab),
                pl.BlockSpec((tm, sn.shape[1]), tab)]
    in_specs += [_const_spec(a.shape) for a in weights]
    sem = ("arbitrary", "arbitrary") if fuse_conv else ("parallel", "parallel")
    return pl.pallas_call(
        functools.partial(_in_proj_body, scale=scale, fuse_conv=fuse_conv),
        grid=(n_seq, tiles), in_specs=in_specs, out_specs=out_specs, out_shape=out_shape,
        scratch_shapes=scratch, compiler_params=_params(*sem), name="in_proj",
    )(x, cs, sn, *weights)


def _lane_tile(x, n):
    return x if n == 1 else jnp.concatenate([x] * n, axis=-1)


def _lane_fold(p, lanes):
    out = p[:, :lanes]
    for c in range(1, p.shape[-1] // lanes):
        out = out + p[:, c * lanes:(c + 1) * lanes]
    return out


def _prompt_attn_body(q_ref, k_ref, wuv_ref, v_ref, s_ref, p_ref, m_ref, l_ref, a_ref, acc_ref, *, tq, tk, rb,
                      groups):
    n_heads, kv_lora, v_dim = wuv_ref.shape
    lanes = m_ref.shape[-1]
    i = pl.program_id(1)
    rows = n_heads * tq
    ratio = tk // tq
    q = q_ref[...].reshape(rows, q_ref.shape[-1])

    m_ref[...] = jnp.full(m_ref.shape, NEG_INF, F32)
    l_ref[...] = jnp.zeros(l_ref.shape, F32)
    acc_ref[...] = jnp.zeros(acc_ref.shape, F32)

    def softmax_block(sl, r0, width, masked):
        s = s_ref[sl, :width]
        if masked:
            q_pos = lax.broadcasted_iota(jnp.int32, (rb, width), 0) + r0 % tq
            k_pos = lax.broadcasted_iota(jnp.int32, (rb, width), 1)
            s = jnp.where(q_pos >= k_pos, s, NEG_INF)
        m_prev = m_ref[sl, :]
        m_new = jnp.maximum(m_prev, jnp.max(s, axis=-1, keepdims=True))
        alpha = jnp.exp2(m_prev - m_new)
        p = jnp.exp2(s - _lane_tile(m_new, width // lanes))
        l_ref[sl, :] = alpha * l_ref[sl, :] + _lane_fold(p, lanes)
        m_ref[sl, :] = m_new
        p_ref[sl, :width] = p.astype(BF16)
        a_ref[sl, :] = alpha

    def step(start, width, masked):
        k = k_ref[0, pl.ds(start, width), :]
        gr = rows // groups
        for g in range(groups):
            s_ref[g * gr:(g + 1) * gr, :width] = _dot_nt(q[g * gr:(g + 1) * gr], k)
        for g in range(groups):
            for r0 in range(g * gr, (g + 1) * gr, rb):
                softmax_block(slice(r0, r0 + rb), r0, width, masked)
            rows_g = slice(g * gr, (g + 1) * gr)
            acc_ref[rows_g, :] = (acc_ref[rows_g, :] * _lane_tile(a_ref[rows_g, :], kv_lora // lanes)
                                  + _dot(p_ref[rows_g, :width], k[:, :kv_lora]))

    def wide_step(j, carry):
        step(pl.multiple_of(j * tk, tk), tk, False)
        return carry

    n_wide = lax.div(i, ratio)
    lax.fori_loop(0, n_wide, wide_step, 0)
    if ratio > 1:
        def narrow_step(j, carry):
            step(pl.multiple_of(n_wide * tk + j * tq, tq), tq, False)
            return carry

        lax.fori_loop(0, lax.rem(i, ratio), narrow_step, 0)
    step(pl.multiple_of(i * tq, tq), tq, True)

    inv_l = 1.0 / jnp.sum(l_ref[...], axis=-1, keepdims=True)
    for h in range(n_heads):
        o_h = acc_ref[h * tq:(h + 1) * tq, :] * inv_l[h * tq:(h + 1) * tq, :]
        v_ref[:, h * v_dim:(h + 1) * v_dim] = _dot(o_h.astype(BF16), wuv_ref[h]).astype(v_ref.dtype)


def _prompt_attention(q, kcat, w_uv, *, batch, seq, tq, tk):
    n_heads, n, dk = q.shape
    kv_lora, v_dim = w_uv.shape[1], w_uv.shape[2]
    nq = seq // tq
    rows = n_heads * tq
    k3 = kcat.reshape(batch, seq, dk)
    return pl.pallas_call(
        functools.partial(_prompt_attn_body, tq=tq, tk=tk, rb=32, groups=4),
        grid=(batch, nq),
        in_specs=[pl.BlockSpec((n_heads, tq, dk), lambda b, i: (0, b * nq + i, 0)),
                  pl.BlockSpec((1, seq, dk), lambda b, i: (b, 0, 0)),
                  _const_spec(w_uv.shape)],
        out_specs=pl.BlockSpec((tq, n_heads * v_dim), lambda b, i: (b * nq + i, 0)),
        out_shape=jax.ShapeDtypeStruct((n, n_heads * v_dim), BF16),
        scratch_shapes=[pltpu.VMEM((rows, tk), F32),
                        pltpu.VMEM((rows, tk), BF16),
                        pltpu.VMEM((rows, LANES), F32),
                        pltpu.VMEM((rows, LANES), F32),
                        pltpu.VMEM((rows, LANES), F32),
                        pltpu.VMEM((rows, kv_lora), F32)],
        compiler_params=_params("parallel", "parallel"), name="prompt_attention",
    )(q, k3, w_uv)


def _sample_attn_body(pt_ref, q_ref, knew_ref, poolc_hbm, poolr_hbm, o_ref,
                      cbuf, rbuf, sem, s_ref, *, n_new, ck, piece):
    n_pages = pt_ref.shape[1]
    page = poolc_hbm.shape[1]
    kv_lora = poolc_hbm.shape[2]
    b = pl.program_id(0)
    n_seq = pl.num_programs(0)
    slot = jnp.bitwise_and(b, 1)

    def page_copies(b_, slot_):
        copies = []
        for p in range(n_pages):
            pid = pt_ref[b_, p]
            copies.append(pltpu.make_async_copy(
                poolc_hbm.at[pid], cbuf.at[slot_, pl.ds(p * page, page), :], sem.at[0, slot_]))
            copies.append(pltpu.make_async_copy(
                poolr_hbm.at[pid], rbuf.at[slot_, :, pl.ds(p * page, page)], sem.at[1, slot_]))
        return copies

    @pl.when(b == 0)
    def _():
        for cp in page_copies(0, 0):
            cp.start()

    @pl.when(b + 1 < n_seq)
    def _():
        for cp in page_copies(b + 1, 1 - slot):
            cp.start()

    for cp in page_copies(b, slot):
        cp.wait()

    q = q_ref[0]
    q_lat = q[:, :kv_lora]
    q_rope = q[:, kv_lora:]

    knew = knew_ref[0]
    s = _dot_nt(q, knew)
    t_q = lax.rem(lax.broadcasted_iota(jnp.int32, s.shape, 0), n_new)
    t_k = lax.broadcasted_iota(jnp.int32, s.shape, 1)
    s = jnp.where(t_k <= t_q, s, NEG_INF)
    m = jnp.max(s, axis=-1, keepdims=True)
    p = jnp.exp2(s - m)
    l = jnp.sum(p, axis=-1, keepdims=True)
    acc = _dot(p.astype(BF16), knew[:, :kv_lora])

    for k0 in range(0, n_pages * page, ck):
        latents = []
        for c0 in range(0, ck, piece):
            cb = cbuf[slot, k0 + c0:k0 + c0 + piece, :].astype(BF16)
            rb = rbuf[slot, :, k0 + c0:k0 + c0 + piece].astype(BF16)
            s_ref[:, c0:c0 + piece] = _dot_nt(q_lat, cb) + _dot(q_rope, rb)
            latents.append(cb)
        s = s_ref[...]
        m_new = jnp.maximum(m, jnp.max(s, axis=-1, keepdims=True))
        alpha = jnp.exp2(m - m_new)
        p = jnp.exp2(s - m_new)
        l = alpha * l + jnp.sum(p, axis=-1, keepdims=True)
        p = p.astype(BF16)
        pv = _dot(p[:, :piece], latents[0])
        for j in range(1, len(latents)):
            pv = pv + _dot(p[:, j * piece:(j + 1) * piece], latents[j])
        acc = alpha * acc + pv
        m = m_new
    o_ref[0] = acc / l


def _sample_attention(q, knew, pool_c, pool_rt, page_table, *, n_new):
    bd, rows, dk = q.shape
    n_pages = page_table.shape[1]
    page, kv_lora = pool_c.shape[1], pool_c.shape[2]
    rope = pool_rt.shape[1]
    past = n_pages * page
    ck = _pick_tile(past, 8192)
    piece = _pick_tile(ck, 1024)
    grid_spec = pltpu.PrefetchScalarGridSpec(
        num_scalar_prefetch=1,
        grid=(bd,),
        in_specs=[pl.BlockSpec((1, rows, dk), lambda b, pt: (b, 0, 0)),
                  pl.BlockSpec((1, knew.shape[1], dk), lambda b, pt: (b, 0, 0)),
                  pl.BlockSpec(memory_space=pl.ANY),
                  pl.BlockSpec(memory_space=pl.ANY)],
        out_specs=pl.BlockSpec((1, rows, kv_lora), lambda b, pt: (b, 0, 0)),
        scratch_shapes=[pltpu.VMEM((2, past, kv_lora), F32),
                        pltpu.VMEM((2, rope, past), F32),
                        pltpu.SemaphoreType.DMA((2, 2)),
                        pltpu.VMEM((rows, ck), F32)])
    return pl.pallas_call(
        functools.partial(_sample_attn_body, n_new=n_new, ck=ck, piece=piece),
        grid_spec=grid_spec,
        out_shape=jax.ShapeDtypeStruct((bd, rows, kv_lora), F32),
        compiler_params=_params("arbitrary"), name="sample_attention",
    )(page_table, q, knew, pool_c, pool_rt)


def _uv_proj_body(o_ref, wuv_ref, v_ref):
    n_heads, _, v_dim = wuv_ref.shape
    for h in range(n_heads):
        v_ref[:, h * v_dim:(h + 1) * v_dim] = _dot(o_ref[h].astype(BF16), wuv_ref[h]).astype(v_ref.dtype)


def _uv_proj(o_lat, w_uv):
    n_heads, n, kv_lora = o_lat.shape
    v_dim = w_uv.shape[2]
    return pl.pallas_call(
        _uv_proj_body,
        grid=(1,),
        in_specs=[_const_spec(o_lat.shape), _const_spec(w_uv.shape)],
        out_specs=_const_spec((n, n_heads * v_dim), single_buffer=False),
        out_shape=jax.ShapeDtypeStruct((n, n_heads * v_dim), BF16),
        compiler_params=_params("arbitrary"), name="uv_proj",
    )(o_lat, w_uv)


def _conv_hist_body(hist_ref, glu_ref, dww_ref, dwb_ref, lng_ref, lnb_ref, uc_ref, ext_ref, u_ref, *, stride):
    for conv_rows in _conv_branch(0, glu_ref[...], hist_ref, dww_ref, dwb_ref, lng_ref, lnb_ref, ext_ref, u_ref,
                                  uc_ref, stride=stride):
        conv_rows()


def _conv_hist(glu, hist, w, *, stride):
    n, c = glu.shape
    pad = _conv_pad(w["conv_dw_w"].shape[0], stride)
    assert hist.shape == (pad, c)
    args = [hist, glu, w["conv_dw_w"], w["conv_dw_b"], w["conv_ln_g"], w["conv_ln_b"]]
    return pl.pallas_call(
        functools.partial(_conv_hist_body, stride=stride),
        grid=(1,), in_specs=[_const_spec(a.shape) for a in args],
        out_specs=_const_spec((n, c), single_buffer=False),
        out_shape=jax.ShapeDtypeStruct((n, c), BF16),
        scratch_shapes=[pltpu.VMEM((pad + n, c), F32), pltpu.VMEM((n, c), F32)],
        compiler_params=_params("arbitrary"), name="conv_hist",
    )(*args)


def _mix_body(uc_ref, v_ref, gate_ref, x_ref, wco_ref, wmo_ref, wmix_ref, ln1g_ref, ln1b_ref, x1_ref, *, alpha):
    d = x_ref.shape[-1]
    y_conv = _dot(uc_ref[...], wco_ref[...])
    y_mla = _dot(v_ref[...], wmo_ref[...])
    gates = gate_ref[...]
    merged = gates[:, :d] * y_conv + gates[:, d:] * y_mla
    mix = _dot(merged.astype(BF16), wmix_ref[...])
    x1_ref[...] = _layer_norm(alpha * x_ref[...] + mix, ln1g_ref[...], ln1b_ref[...])


def _mix(uc, v, gates, x, w, *, tm, alpha):
    n, d = x.shape
    row = lambda i: (i, 0)
    weights = [w["w_conv_out"], w["w_mla_out"], w["w_mix_out"], w["ln1_g"], w["ln1_b"]]
    in_specs = [pl.BlockSpec((tm, uc.shape[1]), row), pl.BlockSpec((tm, v.shape[1]), row),
                pl.BlockSpec((tm, 2 * d), row), pl.BlockSpec((tm, d), row)]
    in_specs += [_const_spec(a.shape) for a in weights]
    return pl.pallas_call(
        functools.partial(_mix_body, alpha=alpha),
        grid=(n // tm,), in_specs=in_specs,
        out_specs=pl.BlockSpec((tm, d), row),
        out_shape=jax.ShapeDtypeStruct((n, d), F32),
        compiler_params=_params("parallel"), name="mix",
    )(uc, v, gates, x, *weights)


def _ffn_body(*refs, stride, pad, chunk, alpha, has_state):
    if has_state:
        state_ref, refs = refs[0], refs[1:]
    (x1_ref, wu_ref, dw_ref, db_ref, wdn_ref, ln2g_ref, ln2b_ref,
     y_ref, tail_ref, hist_ref, eg_ref, ev_ref, act_ref) = refs
    tm = x1_ref.shape[0]
    d_ff = wdn_ref.shape[0]
    width = 2 * d_ff
    taps = dw_ref.shape[0]
    i = pl.program_id(1)

    @pl.when(i == 0)
    def _():
        if has_state:
            for j in range(pad // stride):
                hist_ref[j * stride:(j + 1) * stride, :] = state_ref[:, j * width:(j + 1) * width]
        else:
            hist_ref[...] = jnp.zeros(hist_ref.shape, F32)

    x1 = x1_ref[...]
    x1b = x1.astype(BF16)
    base = pad - (taps - 1) * stride

    def conv_half(e_ref, c0):
        cols = slice(c0, c0 + chunk)
        e_ref[0:pad, :] = hist_ref[:, cols]
        e_ref[pad:pad + tm, :] = _dot(x1b, wu_ref[:, cols])
        hist_ref[:, cols] = e_ref[tm:tm + pad, :]
        h = jnp.broadcast_to(db_ref[:, cols], (tm, chunk))
        for k in range(taps):
            off = base + k * stride
            h = h + dw_ref[k:k + 1, cols] * e_ref[off:off + tm, :]
        return h

    for c0 in range(0, d_ff, chunk):
        hg = conv_half(eg_ref, c0)
        hv = conv_half(ev_ref, d_ff + c0)
        act_ref[:, c0:c0 + chunk] = (hg * jax.nn.sigmoid(hg) * hv).astype(BF16)

    if has_state:
        for j in range(pad // stride):
            tail_ref[:, j * width:(j + 1) * width] = hist_ref[j * stride:(j + 1) * stride, :]
    else:
        tail_ref[0] = hist_ref[...]
    f = _dot(act_ref[...], wdn_ref[...])
    y_ref[...] = _layer_norm(alpha * x1 + f, ln2g_ref[...], ln2b_ref[...])


def _ffn(x1, state, w, *, n_seq, tm, stride, alpha, chunk):
    n, d = x1.shape
    d_ff = w["w_down"].shape[0]
    width = 2 * d_ff
    taps = w["ffn_dw_w"].shape[0]
    pad = _conv_pad(taps, stride)
    tiles = n // n_seq // tm
    row = lambda s, i: (s * tiles + i, 0)
    weights = [w["w_up"], w["ffn_dw_w"], w["ffn_dw_b"], w["w_down"], w["ln2_g"], w["ln2_b"]]
    has_state = state is not None
    in_specs = [pl.BlockSpec((tm, d), row)] + [_const_spec(a.shape) for a in weights]
    args = [x1, *weights]
    if has_state:
        assert tiles == 1 and n_seq == 1 and pad == (taps - 1) * stride
        assert state.shape == (stride, (taps - 1) * width)
        in_specs = [_const_spec(state.shape)] + in_specs
        args = [state] + args
        tail_shape = jax.ShapeDtypeStruct(state.shape, F32)
        tail_spec = _const_spec(state.shape, single_buffer=False)
    else:
        assert tm >= pad
        tail_shape = jax.ShapeDtypeStruct((n_seq, pad, width), F32)
        tail_spec = pl.BlockSpec((1, pad, width), lambda s, i: (s, 0, 0))
    return pl.pallas_call(
        functools.partial(_ffn_body, stride=stride, pad=pad, chunk=chunk, alpha=alpha, has_state=has_state),
        grid=(n_seq, tiles), in_specs=in_specs,
        out_specs=[pl.BlockSpec((tm, d), row), tail_spec],
        out_shape=[jax.ShapeDtypeStruct((n, d), F32), tail_shape],
        scratch_shapes=[pltpu.VMEM((pad, width), F32),
                        pltpu.VMEM((pad + tm, chunk), F32), pltpu.VMEM((pad + tm, chunk), F32),
                        pltpu.VMEM((tm, d_ff), BF16)],
        compiler_params=_params("arbitrary", "arbitrary"), name="conv_ffn",
    )(*args)


def _rope_tables(positions, rope):
    inv = 1.0 / (ROPE_THETA ** (jnp.arange(0, rope, 2, dtype=F32) / rope))
    ang = positions.astype(F32)[:, None] * inv[None, :]
    cos, sin = jnp.cos(ang), jnp.sin(ang)
    reps = max(1, LANES // rope)
    cs = jnp.tile(jnp.concatenate([cos, cos], axis=-1), (1, reps))
    sn = jnp.tile(jnp.concatenate([-sin, sin], axis=-1), (1, reps))
    return cs, sn


def _prepare_weights(l, c_conv, w_in, b_in, conv_dw_w, conv_dw_b, conv_ln_g, conv_ln_b, w_conv_out, q_norm_g,
                     w_uq, kv_norm_g, w_uk, w_uv, w_mla_out, w_mix_out, ln1_g, ln1_b, w_up, ffn_dw_w,
                     ffn_dw_b, w_down, ln2_g, ln2_b, rope):
    kv_lora, n_heads, nope = w_uk.shape[1:]
    q_lora = q_norm_g.shape[1]
    d_ff = w_down.shape[1]
    s_q = 2 * c_conv
    s_kv = s_q + q_lora
    s_kr = s_kv + kv_lora
    s_gate = s_kr + rope
    wi, bi = w_in[l], b_in[l][None, :]
    uq = w_uq[l].reshape(q_lora, n_heads, nope + rope)
    row = lambda a: a[l][None, :]
    return {
        "w_glu": wi[:, :s_q].astype(BF16), "b_glu": bi[:, :s_q],
        "w_q": wi[:, s_q:s_kv].astype(BF16), "b_q": bi[:, s_q:s_kv],
        "w_kv": wi[:, s_kv:s_kr].astype(BF16), "b_kv": bi[:, s_kv:s_kr],
        "w_kr": wi[:, s_kr:s_gate].astype(BF16), "b_kr": bi[:, s_kr:s_gate],
        "w_gate": wi[:, s_gate:].astype(BF16), "b_gate": bi[:, s_gate:],
        "q_norm_g": row(q_norm_g), "kv_norm_g": row(kv_norm_g),
        "w_uq_nope": uq[:, :, :nope].reshape(q_lora, n_heads * nope).astype(BF16),
        "w_uq_rope": uq[:, :, nope:].reshape(q_lora, n_heads * rope).astype(BF16),
        "w_ukT": jnp.transpose(w_uk[l], (1, 2, 0)).astype(BF16),
        "w_uv": jnp.transpose(w_uv[l], (1, 0, 2)).astype(BF16),
        "conv_dw_w": conv_dw_w[l], "conv_dw_b": row(conv_dw_b),
        "conv_ln_g": row(conv_ln_g), "conv_ln_b": row(conv_ln_b),
        "w_conv_out": w_conv_out[l].astype(BF16), "w_mla_out": w_mla_out[l].astype(BF16),
        "w_mix_out": w_mix_out[l].astype(BF16), "ln1_g": row(ln1_g), "ln1_b": row(ln1_b),
        "w_up": w_up[l].astype(BF16), "ffn_dw_w": ffn_dw_w[l], "ffn_dw_b": row(ffn_dw_b),
        "w_down": w_down[l].astype(BF16), "ln2_g": row(ln2_g), "ln2_b": row(ln2_b),
    }


def _pick_tile(n, target):
    t = min(n, target)
    while n % t:
        t -= SUBLANES
    return t


def kernel(x_prompt, x_sample, cache_kv_latent, cache_k_rope, state_conv, state_ffn_conv, page_table, w_in, b_in, conv_dw_w, conv_dw_b, conv_ln_g, conv_ln_b, w_conv_out, q_norm_g, w_uq, kv_norm_g, w_uk, w_uv, w_mla_out, w_mix_out, ln1_g, ln1_b, w_up, ffn_dw_w, ffn_dw_b, w_down, ln2_g, ln2_b):
    depth = w_in.shape[0]
    bp, sp, d = x_prompt.shape
    bs, ts, _ = x_sample.shape
    c_conv = state_conv.shape[-1]
    conv_hist = state_conv.shape[2]
    ffn_hist = state_ffn_conv.shape[2]
    kv_lora, n_heads, nope = w_uk.shape[1:]
    rope = cache_k_rope.shape[-1]
    d_ff = w_down.shape[1]
    page = cache_kv_latent.shape[2]
    n_pages = page_table.shape[1]
    past_len = n_pages * page
    alpha = (2.0 * depth) ** 0.25
    scale = float(nope + rope) ** -0.5 * LOG2_E

    tm_p = _pick_tile(sp, 512)
    tm_ffn = _pick_tile(sp, 512)
    tq_attn = _pick_tile(sp, 256)
    tk_attn = 2 * tq_attn if sp % (2 * tq_attn) == 0 else tq_attn
    n_s = bs * ts
    tm_s = _pick_tile(n_s, 256)
    ffn_chunk = 256 if d_ff % 256 == 0 else 128

    cs_p, sn_p = _rope_tables(jnp.arange(sp, dtype=jnp.int32), rope)
    pos_s = past_len + jnp.arange(ts, dtype=jnp.int32)
    cs_s, sn_s = _rope_tables(jnp.repeat(pos_s, bs), rope)

    h_p = x_prompt.reshape(bp * sp, d)
    h_s = jnp.transpose(x_sample, (1, 0, 2)).reshape(n_s, d)
    outs = [[] for _ in range(8)]
    for l in range(depth):
        w = _prepare_weights(l, c_conv, w_in, b_in, conv_dw_w, conv_dw_b, conv_ln_g, conv_ln_b, w_conv_out,
                             q_norm_g, w_uq, kv_norm_g, w_uk, w_uv, w_mla_out, w_mix_out, ln1_g, ln1_b,
                             w_up, ffn_dw_w, ffn_dw_b, w_down, ln2_g, ln2_b, rope)

        glu, ckv, kr, kcat, q, gates, uc = _in_proj(h_p, cs_p, sn_p, w, n_seq=bp, tm=tm_p, scale=scale,
                                                    fuse_conv=True)
        v = _prompt_attention(q, kcat, w["w_uv"], batch=bp, seq=sp, tq=tq_attn, tk=tk_attn)
        x1 = _mix(uc, v, gates, h_p, w, tm=tm_p, alpha=alpha)
        h_p, tail = _ffn(x1, None, w, n_seq=bp, tm=tm_ffn, stride=1, alpha=alpha, chunk=ffn_chunk)
        outs[0].append(ckv.reshape(bp, sp, kv_lora))
        outs[1].append(kr.reshape(bp, sp, rope))
        outs[2].append(glu.reshape(bp, sp, c_conv)[:, sp - conv_hist:, :])
        outs[3].append(tail[:, -ffn_hist:, :])

        glu, ckv, kr, kcat, q, gates = _in_proj(h_s, cs_s, sn_s, w, n_seq=1, tm=tm_s, scale=scale,
                                                fuse_conv=False)
        q_b = jnp.transpose(q.reshape(n_heads, ts, bs, kv_lora + rope), (2, 0, 1, 3))
        q_b = q_b.reshape(bs, n_heads * ts, kv_lora + rope)
        knew = jnp.transpose(kcat.reshape(ts, bs, kv_lora + rope), (1, 0, 2))
        knew = jnp.pad(knew, ((0, 0), (0, 128 - ts), (0, 0)))
        o_lat = _sample_attention(q_b, knew, cache_kv_latent[l], jnp.swapaxes(cache_k_rope[l], 1, 2), page_table,
                                  n_new=ts)
        o_lat = jnp.transpose(o_lat.reshape(bs, n_heads, ts, kv_lora), (1, 2, 0, 3))
        v = _uv_proj(o_lat.reshape(n_heads, n_s, kv_lora), w["w_uv"])
        conv_prev = jnp.transpose(state_conv[l], (1, 0, 2))
        uc = _conv_hist(glu, conv_prev.reshape(conv_hist * bs, c_conv), w, stride=bs)
        x1 = _mix(uc, v, gates, h_s, w, tm=tm_s, alpha=alpha)
        h_s, tail = _ffn(x1, state_ffn_conv[l].reshape(bs, ffn_hist * 2 * d_ff), w, n_seq=1, tm=n_s,
                         stride=bs, alpha=alpha, chunk=ffn_chunk)
        to_batch_major = lambda a, t: jnp.transpose(a.reshape(t, bs, a.shape[-1]), (1, 0, 2))
        outs[4].append(to_batch_major(ckv, ts))
        outs[5].append(to_batch_major(kr, ts))
        conv_ext = jnp.concatenate([conv_prev, glu.reshape(ts, bs, c_conv)], axis=0)
        outs[6].append(jnp.transpose(conv_ext[-conv_hist:], (1, 0, 2)))
        outs[7].append(tail.reshape(bs, ffn_hist, 2 * d_ff))

    y_prompt = h_p.reshape(bp, sp, d)
    y_sample = jnp.transpose(h_s.reshape(ts, bs, d), (1, 0, 2))
    return (y_prompt, y_sample, *[jnp.stack(o, 0) for o in outs])
```
